```python
import math
import jax, jax.numpy as jnp
from jax import lax
import numpy as np


D_MODEL = 1024
BATCH = 8
SEQ = 8192
DEPTH = 1
DEC_BATCH = 8
DEC_SEQ = 16
PAST_LEN = 2048

CHUNK = 64
GMLP_CHUNK = 128
GMLP_HEADS = 4
GMLP_HEAD_DIM = 128
GMLP_WIDTH = GMLP_HEADS * GMLP_HEAD_DIM
MLA_HEADS = 4
QK_NOPE_DIM = 128
QK_ROPE_DIM = 64
V_HEAD_DIM = 128
Q_LORA_RANK = 384
KV_LORA_RANK = 256
MLA_WIDTH = MLA_HEADS * V_HEAD_DIM
MIX_WIDTH = GMLP_WIDTH + MLA_WIDTH
IN_SPLITS = (GMLP_WIDTH, 2 * GMLP_WIDTH, 2 * GMLP_WIDTH + Q_LORA_RANK,
             2 * GMLP_WIDTH + Q_LORA_RANK + KV_LORA_RANK)
IN_WIDTH = 2 * GMLP_WIDTH + Q_LORA_RANK + KV_LORA_RANK + QK_ROPE_DIM
D_FF = 2816
CONV_W = 3
ROPE_BASE = 10000.0
Q_BLOCK = 128
ATTN_SCALE = 1.0 / math.sqrt(QK_NOPE_DIM + QK_ROPE_DIM)
DEEPNORM_ALPHA = (2.0 * DEPTH) ** 0.25
DEEPNORM_BETA = (8.0 * DEPTH) ** -0.25
LN_EPS = 1e-5
RMS_EPS = 1e-6
NEG_INF = -1e30

kernel_name = 'hymba_gmlp_mla_convffn_deepnorm_stream_step'


def layer_norm(x, g, b):
    xf = x.astype(jnp.float32)
    mu = jnp.mean(xf, axis=-1, keepdims=True)
    var = jnp.mean(jnp.square(xf - mu), axis=-1, keepdims=True)
    y = (xf - mu) * lax.rsqrt(var + LN_EPS)
    return (y * g.astype(jnp.float32) + b.astype(jnp.float32)).astype(x.dtype)


def rms_norm(x, g):
    xf = x.astype(jnp.float32)
    y = xf * lax.rsqrt(jnp.mean(jnp.square(xf), axis=-1, keepdims=True) + RMS_EPS)
    return (y * g.astype(jnp.float32)).astype(x.dtype)


def rope_tables(pos):
    inv_freq = jnp.power(ROPE_BASE, -jnp.arange(0, QK_ROPE_DIM, 2, dtype=jnp.float32) / QK_ROPE_DIM)
    ang = pos.astype(jnp.float32)[:, None] * inv_freq[None, :]
    return jnp.cos(ang), jnp.sin(ang)


def apply_rope(x, cos, sin):
    half = x.shape[-1] // 2
    x1, x2 = x[..., :half], x[..., half:]
    c = cos.astype(x.dtype)
    s = sin.astype(x.dtype)
    return jnp.concatenate([x1 * c - x2 * s, x2 * c + x1 * s], axis=-1)


def gmlp_spatial(v, w_s, b_s):
    bsz, s_len = v.shape[0], v.shape[1]
    n = min(s_len, GMLP_CHUNK)
    blk = jnp.arange(n) // CHUNK
    mask = blk[None, :] <= blk[:, None]
    w = jnp.where(mask[None], w_s[:, :n, :n], 0.0)
    vc = v.reshape(bsz, s_len // n, n, GMLP_HEADS, GMLP_HEAD_DIM)
    s = jnp.einsum('hij,bcjhd->bcihd', w, vc) + b_s[:, :n].T[:, :, None]
    return s.reshape(bsz, s_len, GMLP_HEADS, GMLP_HEAD_DIM)


def mla_attention(q_lat, q_pe, ckv, kpe, q_offset):
    bsz, s_len = q_lat.shape[0], q_lat.shape[1]
    qb = min(s_len, Q_BLOCK)
    nb = s_len // qb
    k_chunk = jnp.arange(ckv.shape[1]) // CHUNK
    q_pos = q_offset + jnp.arange(s_len)

    def attend_block(args):
        ql, qp, qpos = args
        sc = jnp.einsum('bqhc,bkc->bhqk', ql, ckv) + jnp.einsum('bqhr,bkr->bhqk', qp, kpe)
        sc = sc.astype(jnp.float32) * ATTN_SCALE
        visible = k_chunk[None, :] <= (qpos // CHUNK)[:, None]
        sc = jnp.where(visible[None, None], sc, NEG_INF)
        p = jax.nn.softmax(sc, axis=-1).astype(ckv.dtype)
        return jnp.einsum('bhqk,bkc->bqhc', p, ckv)

    xs = (q_lat.reshape(bsz, nb, qb, MLA_HEADS, KV_LORA_RANK).swapaxes(0, 1),
          q_pe.reshape(bsz, nb, qb, MLA_HEADS, QK_ROPE_DIM).swapaxes(0, 1),
          q_pos.reshape(nb, qb))
    o = lax.map(attend_block, xs)
    return o.swapaxes(0, 1).reshape(bsz, s_len, MLA_HEADS, KV_LORA_RANK)


def token_mixer(x, ckv_past, kpe_past, w_in, ln_v_g, ln_v_b, w_s, b_s, g_q, w_uq,
                g_kv, w_uk, w_uv, w_o):
    bsz, s_len = x.shape[0], x.shape[1]
    past = 0 if ckv_past is None else ckv_past.shape[1]
    proj = jnp.einsum('bsd,de->bse', x, w_in)
    u, v, cq, ckv, kpe = jnp.split(proj, list(IN_SPLITS), axis=-1)
    u = jax.nn.gelu(u)
    v = layer_norm(jax.nn.gelu(v), ln_v_g, ln_v_b)
    sgu = gmlp_spatial(v.reshape(bsz, s_len, GMLP_HEADS, GMLP_HEAD_DIM), w_s, b_s)
    gm = u * sgu.reshape(bsz, s_len, GMLP_WIDTH)
    cos, sin = rope_tables(past + jnp.arange(s_len))
    q = jnp.einsum('bsr,rhd->bshd', rms_norm(cq, g_q), w_uq)
    q_nope = q[..., :QK_NOPE_DIM]
    q_pe = apply_rope(q[..., QK_NOPE_DIM:], cos[:, None, :], sin[:, None, :])
    ckv = rms_norm(ckv, g_kv)
    kpe = apply_rope(kpe, cos, sin)
    if ckv_past is None:
        ckv_all, kpe_all = ckv, kpe
    else:
        ckv_all = jnp.concatenate([ckv_past, ckv], axis=1)
        kpe_all = jnp.concatenate([kpe_past, kpe], axis=1)
    q_lat = jnp.einsum('bshd,chd->bshc', q_nope, w_uk)
    o_lat = mla_attention(q_lat, q_pe, ckv_all, kpe_all, past)
    mla = jnp.einsum('bshc,chd->bshd', o_lat, w_uv).reshape(bsz, s_len, MLA_WIDTH)
    out = jnp.einsum('bse,ed->bsd', jnp.concatenate([gm, mla], axis=-1), w_o)
    return out, ckv, kpe, v


def conv_ffn(x, conv_past, w_up, w_conv, b_conv, w_down):
    bsz, s_len = x.shape[0], x.shape[1]
    up = jnp.einsum('bsd,df->bsf', x, w_up)
    if conv_past is None:
        conv_past = jnp.zeros((bsz, CONV_W - 1, up.shape[-1]), up.dtype)
    padded = jnp.concatenate([conv_past, up], axis=1)
    conv = b_conv
    for k in range(CONV_W):
        conv = conv + w_conv[k] * padded[:, k:k + s_len]
    a, g = jnp.split(conv, 2, axis=-1)
    out = jnp.einsum('bsf,fd->bsd', jax.nn.silu(a) * g, w_down)
    return out, padded[:, padded.shape[1] - (CONV_W - 1):]


def trunk_layer(x, ckv_past, kpe_past, conv_past, w_in, ln_v_g, ln_v_b, w_s, b_s, g_q, w_uq,
                g_kv, w_uk, w_uv, w_o, ln1_g, ln1_b, w_up, w_conv, b_conv, w_down, ln2_g, ln2_b):
    mix, ckv, kpe, v = token_mixer(x, ckv_past, kpe_past, w_in, ln_v_g, ln_v_b, w_s, b_s,
                                   g_q, w_uq, g_kv, w_uk, w_uv, w_o)
    h = layer_norm(DEEPNORM_ALPHA * x + mix, ln1_g, ln1_b)
    ff, conv_state = conv_ffn(h, conv_past, w_up, w_conv, b_conv, w_down)
    y = layer_norm(DEEPNORM_ALPHA * h + ff, ln2_g, ln2_b)
    return y, ckv, kpe, v, conv_state


def setup_inputs(seed: int = 0) -> dict:
    key = jax.random.key(seed)
    ks = jax.random.split(key, 26)

    def nrm(k, shape, scale):
        return jax.random.normal(k, shape, jnp.float32) * scale

    L = DEPTH
    return {
        'x_prompt': nrm(ks[0], (BATCH, SEQ, D_MODEL), 1.0),
        'x_sample': nrm(ks[1], (DEC_BATCH, DEC_SEQ, D_MODEL), 1.0),
        'cache_ckv': nrm(ks[2], (L, DEC_BATCH, PAST_LEN, KV_LORA_RANK), 1.0),
        'cache_kpe': nrm(ks[3], (L, DEC_BATCH, PAST_LEN, QK_ROPE_DIM), 1.0),
        'state_ffn_conv': nrm(ks[4], (L, DEC_BATCH, CONV_W - 1, 2 * D_FF), 1.0),
        'w_in': nrm(ks[5], (L, D_MODEL, IN_WIDTH), D_MODEL ** -0.5),
        'ln_v_g': 1.0 + nrm(ks[6], (L, GMLP_WIDTH), 0.02),
        'ln_v_b': nrm(ks[7], (L, GMLP_WIDTH), 0.02),
        'w_s': nrm(ks[8], (L, GMLP_HEADS, GMLP_CHUNK, GMLP_CHUNK), GMLP_CHUNK ** -0.5),
        'b_s': 1.0 + nrm(ks[9], (L, GMLP_HEADS, GMLP_CHUNK), 0.02),
        'g_q': 1.0 + nrm(ks[10], (L, Q_LORA_RANK), 0.02),
        'w_uq': nrm(ks[11], (L, Q_LORA_RANK, MLA_HEADS, QK_NOPE_DIM + QK_ROPE_DIM), Q_LORA_RANK ** -0.5),
        'g_kv': 1.0 + nrm(ks[12], (L, KV_LORA_RANK), 0.02),
        'w_uk': nrm(ks[13], (L, KV_LORA_RANK, MLA_HEADS, QK_NOPE_DIM), KV_LORA_RANK ** -0.5),
        'w_uv': nrm(ks[14], (L, KV_LORA_RANK, MLA_HEADS, V_HEAD_DIM), KV_LORA_RANK ** -0.5),
        'w_o': nrm(ks[15], (L, MIX_WIDTH, D_MODEL), MIX_WIDTH ** -0.5 * DEEPNORM_BETA),
        'ln1_g': 1.0 + nrm(ks[16], (L, D_MODEL), 0.02),
        'ln1_b': nrm(ks[17], (L, D_MODEL), 0.02),
        'w_up': nrm(ks[18], (L, D_MODEL, 2 * D_FF), D_MODEL ** -0.5),
        'w_conv': nrm(ks[19], (L, CONV_W, 2 * D_FF), CONV_W ** -0.5),
        'b_conv': nrm(ks[20], (L, 2 * D_FF), 0.02),
        'w_down': nrm(ks[21], (L, D_FF, D_MODEL), D_FF ** -0.5 * DEEPNORM_BETA),
        'ln2_g': 1.0 + nrm(ks[22], (L, D_MODEL), 0.02),
        'ln2_b': nrm(ks[23], (L, D_MODEL), 0.02),
    }


def reference(x_prompt, x_sample, cache_ckv, cache_kpe, state_ffn_conv, w_in, ln_v_g, ln_v_b,
              w_s, b_s, g_q, w_uq, g_kv, w_uk, w_uv, w_o, ln1_g, ln1_b, w_up, w_conv, b_conv,
              w_down, ln2_g, ln2_b):
    hp, hs = x_prompt, x_sample
    ckv_p, kpe_p, conv_p = [], [], []
    ckv_s, kpe_s, v_s, conv_s = [], [], [], []
    for l in range(DEPTH):
        lw = (w_in[l], ln_v_g[l], ln_v_b[l], w_s[l], b_s[l], g_q[l], w_uq[l], g_kv[l], w_uk[l],
              w_uv[l], w_o[l], ln1_g[l], ln1_b[l], w_up[l], w_conv[l], b_conv[l], w_down[l],
              ln2_g[l], ln2_b[l])
        hp, ckv_new, kpe_new, _, conv_new = trunk_layer(hp, None, None, None, *lw)
        ckv_p.append(ckv_new)
        kpe_p.append(kpe_new)
        conv_p.append(conv_new)
        hs, ckv_new, kpe_new, v_new, conv_new = trunk_layer(hs, cache_ckv[l], cache_kpe[l],
                                                            state_ffn_conv[l], *lw)
        ckv_s.append(ckv_new)
        kpe_s.append(kpe_new)
        v_s.append(v_new)
        conv_s.append(conv_new)
    return (hp, hs, jnp.stack(ckv_p), jnp.stack(kpe_p), jnp.stack(conv_p),
            jnp.stack(ckv_s), jnp.stack(kpe_s), jnp.stack(v_s), jnp.stack(conv_s))
```

```python
import functools
import math

import jax
import jax.numpy as jnp
from jax import lax
from jax.experimental import pallas as pl
from jax.experimental.pallas import tpu as pltpu

CHUNK = 64
GMLP_CHUNK = 128
GMLP_HEADS = 4
GMLP_HEAD_DIM = 128
GMLP_WIDTH = GMLP_HEADS * GMLP_HEAD_DIM
MLA_HEADS = 4
QK_NOPE_DIM = 128
QK_ROPE_DIM = 64
QK_DIM = QK_NOPE_DIM + QK_ROPE_DIM
V_HEAD_DIM = 128
MLA_WIDTH = MLA_HEADS * V_HEAD_DIM
Q_LORA_RANK = 384
KV_LORA_RANK = 256
CONV_W = 3
ROPE_BASE = 10000.0
ATTN_SCALE = 1.0 / math.sqrt(QK_NOPE_DIM + QK_ROPE_DIM)
LN_EPS = 1e-5
RMS_EPS = 1e-6
NEG_INF = -1e30

Q_EXT_DIM = QK_NOPE_DIM + 2 * QK_ROPE_DIM

V7X_VMEM_LIMIT_BYTES = 56 * 1024 * 1024

F32 = jnp.float32
BF16 = jnp.bfloat16


def _layer_norm(x, g, b):
    mu = jnp.mean(x, axis=-1, keepdims=True)
    xc = x - mu
    var = jnp.mean(xc * xc, axis=-1, keepdims=True)
    return xc * lax.rsqrt(var + LN_EPS) * g + b


def _rms_norm(x, g):
    return x * lax.rsqrt(jnp.mean(x * x, axis=-1, keepdims=True) + RMS_EPS) * g


def _chunk_of(pos):
    return jnp.right_shift(pos, CHUNK.bit_length() - 1)


def _rope(pair, table):
    t = pair * table
    return t + pltpu.roll(t, QK_ROPE_DIM, 1)


def _dot(a, b):
    return jnp.dot(a, b, preferred_element_type=F32)


def _dot_nt(a, b):
    return lax.dot_general(a, b, (((1,), (1,)), ((), ())), preferred_element_type=F32)


def _proj_kernel(x_ref, w_in_ref, lnv_g_ref, lnv_b_ref, wmix_ref, bmix_ref, gq_ref, wuq_ref,
                 gkv_ref, wuk_ref, wuv_ref, rope_ref,
                 gm_ref, q_ref, k_ref, v_ref, ckv_ref, kpe_ref, vln_ref, *, tm, seg):
    x = x_ref[...].astype(BF16)
    proj = _dot(x, w_in_ref[...])
    o_v, o_cq = GMLP_WIDTH, 2 * GMLP_WIDTH
    o_ckv = o_cq + Q_LORA_RANK
    o_kpe = o_ckv + KV_LORA_RANK

    u = jax.nn.gelu(proj[:, 0:o_v])
    v = _layer_norm(jax.nn.gelu(proj[:, o_v:o_cq]), lnv_g_ref[...], lnv_b_ref[...])
    vln_ref[...] = v
    vb = v.astype(BF16)
    row = lax.broadcasted_iota(jnp.int32, (GMLP_CHUNK, GMLP_CHUNK), 0)
    col = lax.broadcasted_iota(jnp.int32, (GMLP_CHUNK, GMLP_CHUNK), 1)
    same_chunk = jnp.bitwise_xor(row, col) < seg
    keep = same_chunk & (_chunk_of(jnp.bitwise_and(col, seg - 1)) <= _chunk_of(jnp.bitwise_and(row, seg - 1)))
    for h in range(GMLP_HEADS):
        cs = slice(h * GMLP_HEAD_DIM, (h + 1) * GMLP_HEAD_DIM)
        w_h = jnp.where(keep, wmix_ref[h], 0.0).astype(BF16)
        for c in range(tm // GMLP_CHUNK):
            rs = slice(c * GMLP_CHUNK, (c + 1) * GMLP_CHUNK)
            s = _dot(w_h, vb[rs, cs]) + bmix_ref[:, cs]
            gm_ref[rs, cs] = (u[rs, cs] * s).astype(BF16)

    rope = rope_ref[...]

    cqn = _rms_norm(proj[:, o_cq:o_ckv], gq_ref[...]).astype(BF16)
    q = _dot(cqn, wuq_ref[...])
    for h in range(MLA_HEADS):
        b0 = h * Q_EXT_DIM
        q_pe = _rope(q[:, b0 + QK_NOPE_DIM:b0 + Q_EXT_DIM], rope)[:, 0:QK_ROPE_DIM]
        q_ref[h, :, 0:QK_NOPE_DIM] = (q[:, b0:b0 + QK_NOPE_DIM] * ATTN_SCALE).astype(BF16)
        q_ref[h, :, QK_NOPE_DIM:QK_DIM] = (q_pe * ATTN_SCALE).astype(BF16)

    ckv = _rms_norm(proj[:, o_ckv:o_kpe], gkv_ref[...])
    ckv_ref[...] = ckv
    kpe = _rope(proj[:, o_kpe:o_kpe + 2 * QK_ROPE_DIM], rope)[:, 0:QK_ROPE_DIM]
    kpe_ref[...] = kpe
    cb = ckv.astype(BF16)
    k_nope = _dot(cb, wuk_ref[...])
    v_up = _dot(cb, wuv_ref[...])
    kpe_b = kpe.astype(BF16)
    for h in range(MLA_HEADS):
        k_ref[h, :, 0:QK_NOPE_DIM] = k_nope[:, h * QK_NOPE_DIM:(h + 1) * QK_NOPE_DIM].astype(BF16)
        k_ref[h, :, QK_NOPE_DIM:QK_DIM] = kpe_b
        v_ref[h] = v_up[:, h * V_HEAD_DIM:(h + 1) * V_HEAD_DIM].astype(BF16)


def _proj_call(x, w_in_ext, lnv_g, lnv_b, wmix, bmix, g_q, wuq_ext, g_kv, wuk, wuv, rope_t,
               *, tm, seg):
    n, d = x.shape
    in_w = w_in_ext.shape[1]
    n_tiles = n // tm
    tbl_tiles = rope_t.shape[0] // tm
    const = lambda *shape: pl.BlockSpec(shape, lambda i: (0,) * len(shape))
    in_specs = [
        pl.BlockSpec((tm, d), lambda i: (i, 0)),
        const(d, in_w),
        const(1, GMLP_WIDTH), const(1, GMLP_WIDTH),
        const(GMLP_HEADS, GMLP_CHUNK, GMLP_CHUNK), const(GMLP_CHUNK, GMLP_WIDTH),
        const(1, Q_LORA_RANK), const(Q_LORA_RANK, MLA_HEADS * Q_EXT_DIM),
        const(1, KV_LORA_RANK), const(KV_LORA_RANK, MLA_HEADS * QK_NOPE_DIM),
        const(KV_LORA_RANK, MLA_WIDTH),
        pl.BlockSpec((tm, 2 * QK_ROPE_DIM), lambda i: (i % tbl_tiles, 0)),
    ]
    out_shape = [
        jax.ShapeDtypeStruct((n, GMLP_WIDTH), BF16),
        jax.ShapeDtypeStruct((MLA_HEADS, n, QK_DIM), BF16),
        jax.ShapeDtypeStruct((MLA_HEADS, n, QK_DIM), BF16),
        jax.ShapeDtypeStruct((MLA_HEADS, n, V_HEAD_DIM), BF16),
        jax.ShapeDtypeStruct((n, KV_LORA_RANK), F32),
        jax.ShapeDtypeStruct((n, QK_ROPE_DIM), F32),
        jax.ShapeDtypeStruct((n, GMLP_WIDTH), F32),
    ]
    out_specs = [
        pl.BlockSpec((tm, GMLP_WIDTH), lambda i: (i, 0)),
        pl.BlockSpec((MLA_HEADS, tm, QK_DIM), lambda i: (0, i, 0)),
        pl.BlockSpec((MLA_HEADS, tm, QK_DIM), lambda i: (0, i, 0)),
        pl.BlockSpec((MLA_HEADS, tm, V_HEAD_DIM), lambda i: (0, i, 0)),
        pl.BlockSpec((tm, KV_LORA_RANK), lambda i: (i, 0)),
        pl.BlockSpec((tm, QK_ROPE_DIM), lambda i: (i, 0)),
        pl.BlockSpec((tm, GMLP_WIDTH), lambda i: (i, 0)),
    ]
    return pl.pallas_call(
        functools.partial(_proj_kernel, tm=tm, seg=seg),
        grid=(n_tiles,),
        in_specs=in_specs,
        out_specs=out_specs,
        out_shape=out_shape,
        compiler_params=pltpu.CompilerParams(
            dimension_semantics=("parallel",), vmem_limit_bytes=V7X_VMEM_LIMIT_BYTES),
        name=f"proj_tm{tm}",
    )(x, w_in_ext, lnv_g, lnv_b, wmix, bmix, g_q, wuq_ext, g_kv, wuk, wuv, rope_t)


def _attn_kernel(q_ref, k_ref, v_ref, o_ref, *, tq):
    qi = pl.program_id(2)
    q = q_ref[0]

    def step(k0, carry, masked):
        m, l, acc = carry
        k = k_ref[0, pl.ds(k0, tq), :]
        v = v_ref[0, pl.ds(k0, tq), :]
        s = _dot_nt(q, k)
        if masked:
            qc = _chunk_of(lax.broadcasted_iota(jnp.int32, (tq, tq), 0))
            kc = _chunk_of(lax.broadcasted_iota(jnp.int32, (tq, tq), 1))
            s = jnp.where(kc <= qc, s, NEG_INF)
        m_new = jnp.maximum(m, jnp.max(s, axis=-1, keepdims=True))
        alpha = jnp.exp(m - m_new)
        p = jnp.exp(s - m_new)
        l = alpha * l + jnp.sum(p, axis=-1, keepdims=True)
        acc = alpha * acc + _dot(p.astype(BF16), v)
        return m_new, l, acc

    init = (jnp.full((tq, 1), NEG_INF, F32), jnp.zeros((tq, 1), F32),
            jnp.zeros((tq, V_HEAD_DIM), F32))
    carry = lax.fori_loop(
        0, qi, lambda j, c: step(pl.multiple_of(j * tq, tq), c, False), init)
    _, l, acc = step(pl.multiple_of(qi * tq, tq), carry, True)
    o_ref[...] = (acc / l).astype(o_ref.dtype)


def _attn_call(q, k, v, *, batch, seq, tq):
    n = batch * seq
    nq = seq // tq
    return pl.pallas_call(
        functools.partial(_attn_kernel, tq=tq),
        grid=(batch, MLA_HEADS, nq),
        in_specs=[
            pl.BlockSpec((1, tq, QK_DIM), lambda b, h, i: (h, b * nq + i, 0)),
            pl.BlockSpec((1, seq, QK_DIM), lambda b, h, i: (h, b, 0)),
            pl.BlockSpec((1, seq, V_HEAD_DIM), lambda b, h, i: (h, b, 0)),
        ],
        out_specs=pl.BlockSpec((tq, V_HEAD_DIM), lambda b, h, i: (b * nq + i, h)),
        out_shape=jax.ShapeDtypeStruct((n, MLA_WIDTH), BF16),
        compiler_params=pltpu.CompilerParams(
            dimension_semantics=("parallel", "parallel", "arbitrary"),
            vmem_limit_bytes=V7X_VMEM_LIMIT_BYTES),
        name="attn_prompt",
    )(q, k, v)


def _attn_cache_kernel(q_ref, cckv_ref, ckpe_ref, nckv_ref, nkpe_ref, wuk_ref, wuv_ref, o_ref,
                       *, sq, past):
    ckv_c = cckv_ref[0].astype(BF16)
    kpe_c = ckpe_ref[0].astype(BF16)
    ckv_n = nckv_ref[...].astype(BF16)
    kpe_n = nkpe_ref[...].astype(BF16)
    q_chunk = _chunk_of(past + lax.broadcasted_iota(jnp.int32, (sq, 1), 0))
    vis_c = _chunk_of(lax.broadcasted_iota(jnp.int32, (sq, past), 1)) <= q_chunk
    vis_n = _chunk_of(past + lax.broadcasted_iota(jnp.int32, (sq, sq), 1)) <= q_chunk
    for h in range(MLA_HEADS):
        qh = q_ref[h]
        wuk_h = wuk_ref[:, h * QK_NOPE_DIM:(h + 1) * QK_NOPE_DIM]
        q_lat = _dot_nt(qh[:, 0:QK_NOPE_DIM], wuk_h).astype(BF16)
        q_pe = qh[:, QK_NOPE_DIM:QK_DIM]
        s_c = jnp.where(vis_c, _dot_nt(q_lat, ckv_c) + _dot_nt(q_pe, kpe_c), NEG_INF)
        s_n = jnp.where(vis_n, _dot_nt(q_lat, ckv_n) + _dot_nt(q_pe, kpe_n), NEG_INF)
        m = jnp.maximum(jnp.max(s_c, axis=-1, keepdims=True), jnp.max(s_n, axis=-1, keepdims=True))
        p_c = jnp.exp(s_c - m)
        p_n = jnp.exp(s_n - m)
        l = jnp.sum(p_c, axis=-1, keepdims=True) + jnp.sum(p_n, axis=-1, keepdims=True)
        o_lat = (_dot(p_c.astype(BF16), ckv_c) + _dot(p_n.astype(BF16), ckv_n)) / l
        wuv_h = wuv_ref[:, h * V_HEAD_DIM:(h + 1) * V_HEAD_DIM]
        o_ref[:, h * V_HEAD_DIM:(h + 1) * V_HEAD_DIM] = _dot(o_lat.astype(BF16), wuv_h).astype(o_ref.dtype)


def _attn_cache_call(q, cache_ckv, cache_kpe, new_ckv, new_kpe, wuk, wuv, *, batch, sq):
    past = cache_ckv.shape[1]
    return pl.pallas_call(
        functools.partial(_attn_cache_kernel, sq=sq, past=past),
        grid=(batch,),
        in_specs=[
            pl.BlockSpec((MLA_HEADS, sq, QK_DIM), lambda b: (0, b, 0)),
            pl.BlockSpec((1, past, KV_LORA_RANK), lambda b: (b, 0, 0)),
            pl.BlockSpec((1, past, QK_ROPE_DIM), lambda b: (b, 0, 0)),
            pl.BlockSpec((sq, KV_LORA_RANK), lambda b: (b, 0)),
            pl.BlockSpec((sq, QK_ROPE_DIM), lambda b: (b, 0)),
            pl.BlockSpec(wuk.shape, lambda b: (0, 0)),
            pl.BlockSpec(wuv.shape, lambda b: (0, 0)),
        ],
        out_specs=pl.BlockSpec((sq, MLA_WIDTH), lambda b: (b, 0)),
        out_shape=jax.ShapeDtypeStruct((batch * sq, MLA_WIDTH), BF16),
        compiler_params=pltpu.CompilerParams(
            dimension_semantics=("parallel",), vmem_limit_bytes=V7X_VMEM_LIMIT_BYTES),
        name="attn_cache",
    )(q, cache_ckv, cache_kpe, new_ckv, new_kpe, wuk, wuv)


def _ffn_kernel(x_ref, gm_ref, mla_ref, cpast_ref, wo_ref, ln1g_ref, ln1b_ref, wup_ref, wconv_ref,
                bconv_ref, wdown_ref, ln2g_ref, ln2b_ref, y_ref, cstate_ref, carry_ref, act_ref,
                *, tm, tiles_per_seg, d_ff, cw, alpha):
    i = pl.program_id(0)

    @pl.when(i % tiles_per_seg == 0)
    def _():
        carry_ref[...] = cpast_ref[0]

    mix = _dot(gm_ref[...], wo_ref[0:GMLP_WIDTH, :]) + _dot(mla_ref[...], wo_ref[GMLP_WIDTH:, :])
    h = _layer_norm(alpha * x_ref[...] + mix, ln1g_ref[...], ln1b_ref[...])
    hb = h.astype(BF16)

    row = lax.broadcasted_iota(jnp.int32, (tm, cw), 0)

    def causal_conv(up, off):
        cols = slice(off, off + cw)
        prev2 = carry_ref[0:1, cols]
        prev1 = carry_ref[1:2, cols]
        s1 = jnp.where(row == 0, prev1, pltpu.roll(up, 1, 0))
        s2 = jnp.where(row == 0, prev2, jnp.where(row == 1, prev1, pltpu.roll(up, 2, 0)))
        tail = up[tm - (CONV_W - 1):tm, :]
        carry_ref[:, cols] = tail
        cstate_ref[0, :, cols] = tail
        conv = bconv_ref[:, cols] + wconv_ref[0:1, cols] * s2
        conv = conv + wconv_ref[1:2, cols] * s1
        return conv + wconv_ref[2:3, cols] * up

    for c in range(d_ff // cw):
        a = causal_conv(_dot(hb, wup_ref[:, c * cw:(c + 1) * cw]), c * cw)
        g = causal_conv(_dot(hb, wup_ref[:, d_ff + c * cw:d_ff + (c + 1) * cw]), d_ff + c * cw)
        act_ref[:, c * cw:(c + 1) * cw] = (a * (1.0 / (1.0 + jnp.exp(-a))) * g).astype(BF16)

    ff = _dot(act_ref[...], wdown_ref[...])
    y_ref[...] = _layer_norm(alpha * h + ff, ln2g_ref[...], ln2b_ref[...])


def _ffn_call(x, gm, mla, conv_past, wo, ln1g, ln1b, wup, wconv, bconv, wdown, ln2g, ln2b,
              *, tm, seq, alpha):
    n, d = x.shape
    d_ff = wdown.shape[0]
    batch = n // seq
    tiles_per_seg = seq // tm
    cw = 256
    const = lambda *shape: pl.BlockSpec(shape, lambda i: (0,) * len(shape),
                                        pipeline_mode=pl.Buffered(1))
    return pl.pallas_call(
        functools.partial(_ffn_kernel, tm=tm, tiles_per_seg=tiles_per_seg, d_ff=d_ff, cw=cw,
                          alpha=alpha),
        grid=(n // tm,),
        in_specs=[
            pl.BlockSpec((tm, d), lambda i: (i, 0)),
            pl.BlockSpec((tm, GMLP_WIDTH), lambda i: (i, 0)),
            pl.BlockSpec((tm, MLA_WIDTH), lambda i: (i, 0)),
            pl.BlockSpec((1, CONV_W - 1, 2 * d_ff), lambda i: (i // tiles_per_seg, 0, 0)),
            const(GMLP_WIDTH + MLA_WIDTH, d),
            const(1, d), const(1, d),
            const(d, 2 * d_ff),
            const(CONV_W, 2 * d_ff), const(1, 2 * d_ff),
            const(d_ff, d),
            const(1, d), const(1, d),
        ],
        out_specs=[
            pl.BlockSpec((tm, d), lambda i: (i, 0)),
            pl.BlockSpec((1, CONV_W - 1, 2 * d_ff), lambda i: (i // tiles_per_seg, 0, 0)),
        ],
        out_shape=[
            jax.ShapeDtypeStruct((n, d), F32),
            jax.ShapeDtypeStruct((batch, CONV_W - 1, 2 * d_ff), F32),
        ],
        scratch_shapes=[
            pltpu.VMEM((CONV_W - 1, 2 * d_ff), F32),
            pltpu.VMEM((tm, d_ff), BF16),
        ],
        compiler_params=pltpu.CompilerParams(
            dimension_semantics=("arbitrary",), vmem_limit_bytes=V7X_VMEM_LIMIT_BYTES),
        name=f"ffn_tm{tm}",
    )(x, gm, mla, conv_past, wo, ln1g, ln1b, wup, wconv, bconv, wdown, ln2g, ln2b)


def _rope_tables(pos):
    inv_freq = jnp.power(ROPE_BASE, -jnp.arange(0, QK_ROPE_DIM, 2, dtype=F32) / QK_ROPE_DIM)
    ang = pos.astype(F32)[:, None] * inv_freq[None, :]
    cos, sin = jnp.cos(ang), jnp.sin(ang)
    return jnp.concatenate([cos, cos, -sin, sin], axis=-1)


def _swap_halves(w):
    half = w.shape[-1] // 2
    return jnp.concatenate([w[..., half:], w[..., :half]], axis=-1)


def _layer_weights(l, w_in, ln_v_g, ln_v_b, g_q, w_uq, g_kv, w_uk, w_uv, w_o, ln1_g, ln1_b, w_up,
                   w_conv, b_conv, w_down, ln2_g, ln2_b):
    row = lambda a: a[l].reshape(1, -1)
    o_kpe = 2 * GMLP_WIDTH + Q_LORA_RANK + KV_LORA_RANK
    w_in_ext = jnp.concatenate([w_in[l], _swap_halves(w_in[l][:, o_kpe:])], axis=1).astype(BF16)
    wq = w_uq[l]
    wuq_ext = jnp.concatenate([wq, _swap_halves(wq[..., QK_NOPE_DIM:])], axis=-1)
    wuq_ext = wuq_ext.reshape(Q_LORA_RANK, MLA_HEADS * Q_EXT_DIM).astype(BF16)
    return dict(
        w_in_ext=w_in_ext, lnv_g=row(ln_v_g), lnv_b=row(ln_v_b), g_q=row(g_q), wuq_ext=wuq_ext,
        g_kv=row(g_kv),
        wuk=w_uk[l].reshape(KV_LORA_RANK, MLA_HEADS * QK_NOPE_DIM).astype(BF16),
        wuv=w_uv[l].reshape(KV_LORA_RANK, MLA_WIDTH).astype(BF16),
        wo=w_o[l].astype(BF16), ln1g=row(ln1_g), ln1b=row(ln1_b), wup=w_up[l].astype(BF16),
        wconv=w_conv[l], bconv=row(b_conv), wdown=w_down[l].astype(BF16), ln2g=row(ln2_g),
        ln2b=row(ln2_b))


def _mix_operands(w_s, b_s, seg):
    reps = GMLP_CHUNK // seg
    wmix = jnp.tile(w_s[:, :seg, :seg], (1, reps, reps))
    bcol = jnp.tile(b_s[:, :seg].T, (reps, 1))
    return wmix, jnp.repeat(bcol, GMLP_HEAD_DIM, axis=1)


def _trunk(x, w, w_s, b_s, *, batch, seq, past, conv_past, alpha, tm_proj, tm_ffn, attend):
    seg = min(seq, GMLP_CHUNK)
    wmix, bmix = _mix_operands(w_s, b_s, seg)
    rope_t = _rope_tables(past + jnp.arange(seq))
    if seq < tm_proj:
        rope_t = jnp.tile(rope_t, (tm_proj // seq, 1))
    gm, q, k, v, ckv, kpe, vln = _proj_call(
        x, w["w_in_ext"], w["lnv_g"], w["lnv_b"], wmix, bmix, w["g_q"], w["wuq_ext"], w["g_kv"],
        w["wuk"], w["wuv"], rope_t, tm=tm_proj, seg=seg)
    mla = attend(q, k, v, ckv, kpe)
    y, conv_state = _ffn_call(x, gm, mla, conv_past, w["wo"], w["ln1g"], w["ln1b"], w["wup"],
                              w["wconv"], w["bconv"], w["wdown"], w["ln2g"], w["ln2b"],
                              tm=tm_ffn, seq=seq, alpha=alpha)
    return y, ckv, kpe, vln, conv_state


def kernel(x_prompt, x_sample, cache_ckv, cache_kpe, state_ffn_conv, w_in, ln_v_g, ln_v_b, w_s, b_s, g_q, w_uq, g_kv, w_uk, w_uv, w_o, ln1_g, ln1_b, w_up, w_conv, b_conv, w_down, ln2_g, ln2_b):
    depth = w_in.shape[0]
    alpha = (2.0 * depth) ** 0.25
    bp, sp, d = x_prompt.shape
    bs, ss, _ = x_sample.shape
    d_ff = w_down.shape[1]
    hp = x_prompt.reshape(bp * sp, d)
    hs = x_sample.reshape(bs * ss, d)
    outs = [[] for _ in range(7)]
    for l in range(depth):
        w = _layer_weights(l, w_in, ln_v_g, ln_v_b, g_q, w_uq, g_kv, w_uk, w_uv, w_o, ln1_g, ln1_b,
                           w_up, w_conv, b_conv, w_down, ln2_g, ln2_b)
        hp, ckv_p, kpe_p, _, conv_p = _trunk(
            hp, w, w_s[l], b_s[l], batch=bp, seq=sp, past=0,
            conv_past=jnp.zeros((bp, CONV_W - 1, 2 * d_ff), F32), alpha=alpha,
            tm_proj=512, tm_ffn=512,
            attend=lambda q, k, v, ckv, kpe: _attn_call(q, k, v, batch=bp, seq=sp, tq=512))
        hs, ckv_s, kpe_s, v_s, conv_s = _trunk(
            hs, w, w_s[l], b_s[l], batch=bs, seq=ss, past=cache_ckv.shape[2],
            conv_past=state_ffn_conv[l], alpha=alpha, tm_proj=bs * ss, tm_ffn=ss,
            attend=lambda q, k, v, ckv, kpe: _attn_cache_call(
                q, cache_ckv[l], cache_kpe[l], ckv, kpe, w["wuk"], w["wuv"], batch=bs, sq=ss))
        for acc, val in zip(outs, (
                ckv_p.reshape(bp, sp, -1), kpe_p.reshape(bp, sp, -1), conv_p,
                ckv_s.reshape(bs, ss, -1), kpe_s.reshape(bs, ss, -1), v_s.reshape(bs, ss, -1), conv_s)):
            acc.append(val)
    return (hp.reshape(bp, sp, d), hs.reshape(bs, ss, d), *(jnp.stack(o) for o in outs))
```

```python
import functools
import math

import jax
import jax.numpy as jnp
from jax import lax
from jax.experimental import pallas as pl
from jax.experimental.pallas import tpu as pltpu

CHUNK = 64
GMLP_CHUNK = 128
GMLP_HEADS = 4
GMLP_HEAD_DIM = 128
GMLP_WIDTH = GMLP_HEADS * GMLP_HEAD_DIM
MLA_HEADS = 4
QK_NOPE_DIM = 128
QK_ROPE_DIM = 64
QK_DIM = QK_NOPE_DIM + QK_ROPE_DIM
V_HEAD_DIM = 128
MLA_WIDTH = MLA_HEADS * V_HEAD_DIM
Q_LORA_RANK = 384
KV_LORA_RANK = 256
CONV_W = 3
ROPE_BASE = 10000.0
ATTN_SCALE = 1.0 / math.sqrt(QK_NOPE_DIM + QK_ROPE_DIM)
Q_SCALE = ATTN_SCALE * math.log2(math.e)
LN_EPS = 1e-5
RMS_EPS = 1e-6
NEG_INF = -1e30

Q_EXT_DIM = QK_NOPE_DIM + 2 * QK_ROPE_DIM

V7X_VMEM_LIMIT_BYTES = 56 * 1024 * 1024
LANES = 128
assert V_HEAD_DIM == LANES
PROMPT_TILE = 512
ATTN_QUERY_TILE = 1024
ATTN_HEADS_PER_STEP = 2
ATTN_ROW_BLOCK = 64

F32 = jnp.float32
BF16 = jnp.bfloat16


def _layer_norm(x, g, b):
    mu = jnp.mean(x, axis=-1, keepdims=True)
    xc = x - mu
    var = jnp.mean(xc * xc, axis=-1, keepdims=True)
    return xc * lax.rsqrt(var + LN_EPS) * g + b


def _rms_norm(x, g):
    return x * lax.rsqrt(jnp.mean(x * x, axis=-1, keepdims=True) + RMS_EPS) * g


def _chunk_of(pos):
    return jnp.right_shift(pos, CHUNK.bit_length() - 1)


def _rope(pair, table):
    t = pair * table
    return t + pltpu.roll(t, QK_ROPE_DIM, 1)


def _dot(a, b):
    return jnp.dot(a, b, preferred_element_type=F32)


def _dot_nt(a, b):
    return lax.dot_general(a, b, (((1,), (1,)), ((), ())), preferred_element_type=F32)


def _proj_kernel(x_ref, w_in_ref, lnv_g_ref, lnv_b_ref, wmix_ref, bmix_ref, gq_ref, wuq_ref,
                 gkv_ref, wukt_ref, wuv_ref, rope_ref,
                 gm_ref, q_ref, kt_ref, v_ref, ckv_ref, kpe_ref, *maybe_vln_ref, tm, seg):
    x = x_ref[...].astype(BF16)
    proj = _dot(x, w_in_ref[...])
    o_v, o_cq = GMLP_WIDTH, 2 * GMLP_WIDTH
    o_ckv = o_cq + Q_LORA_RANK
    o_kpe = o_ckv + KV_LORA_RANK

    u = jax.nn.gelu(proj[:, 0:o_v])
    v = _layer_norm(jax.nn.gelu(proj[:, o_v:o_cq]), lnv_g_ref[...], lnv_b_ref[...])
    for vln_ref in maybe_vln_ref:
        vln_ref[...] = v
    vb = v.astype(BF16)
    row = lax.broadcasted_iota(jnp.int32, (GMLP_CHUNK, GMLP_CHUNK), 0)
    col = lax.broadcasted_iota(jnp.int32, (GMLP_CHUNK, GMLP_CHUNK), 1)
    same_chunk = jnp.bitwise_xor(row, col) < seg
    keep = same_chunk & (_chunk_of(jnp.bitwise_and(col, seg - 1)) <= _chunk_of(jnp.bitwise_and(row, seg - 1)))
    for h in range(GMLP_HEADS):
        cs = slice(h * GMLP_HEAD_DIM, (h + 1) * GMLP_HEAD_DIM)
        w_h = jnp.where(keep, wmix_ref[h], 0.0).astype(BF16)
        for c in range(tm // GMLP_CHUNK):
            rs = slice(c * GMLP_CHUNK, (c + 1) * GMLP_CHUNK)
            s = _dot(w_h, vb[rs, cs]) + bmix_ref[:, cs]
            gm_ref[rs, cs] = (u[rs, cs] * s).astype(BF16)

    rope = rope_ref[...]

    cqn = _rms_norm(proj[:, o_cq:o_ckv], gq_ref[...]).astype(BF16)
    q = _dot(cqn, wuq_ref[...])
    for h in range(MLA_HEADS):
        b0 = h * Q_EXT_DIM
        q_pe = _rope(q[:, b0 + QK_NOPE_DIM:b0 + Q_EXT_DIM], rope)[:, 0:QK_ROPE_DIM]
        q_ref[h, :, 0:QK_NOPE_DIM] = (q[:, b0:b0 + QK_NOPE_DIM] * Q_SCALE).astype(BF16)
        q_ref[h, :, QK_NOPE_DIM:QK_DIM] = (q_pe * Q_SCALE).astype(BF16)

    ckv = _rms_norm(proj[:, o_ckv:o_kpe], gkv_ref[...])
    ckv_ref[...] = ckv
    kpe2 = _rope(proj[:, o_kpe:o_kpe + 2 * QK_ROPE_DIM], rope)
    kpe_ref[...] = kpe2[:, 0:QK_ROPE_DIM]
    cb = ckv.astype(BF16)
    k_nope_t = _dot_nt(wukt_ref[...], cb)
    kpe_t = kpe2.T[0:QK_ROPE_DIM, :].astype(BF16)
    v_up = _dot(cb, wuv_ref[...])
    for h in range(MLA_HEADS):
        kt_ref[h, 0, 0:QK_NOPE_DIM, :] = k_nope_t[h * QK_NOPE_DIM:(h + 1) * QK_NOPE_DIM, :].astype(BF16)
        kt_ref[h, 0, QK_NOPE_DIM:QK_DIM, :] = kpe_t
        v_ref[h] = v_up[:, h * V_HEAD_DIM:(h + 1) * V_HEAD_DIM].astype(BF16)


def _proj_call(x, w_in_ext, lnv_g, lnv_b, wmix, bmix, g_q, wuq_ext, g_kv, wukt, wuv, rope_t,
               *, tm, seg, emit_vln):
    n, d = x.shape
    in_w = w_in_ext.shape[1]
    n_tiles = n // tm
    tbl_tiles = rope_t.shape[0] // tm
    const = lambda *shape: pl.BlockSpec(shape, lambda i: (0,) * len(shape))
    in_specs = [
        pl.BlockSpec((tm, d), lambda i: (i, 0)),
        const(d, in_w),
        const(1, GMLP_WIDTH), const(1, GMLP_WIDTH),
        const(GMLP_HEADS, GMLP_CHUNK, GMLP_CHUNK), const(GMLP_CHUNK, GMLP_WIDTH),
        const(1, Q_LORA_RANK), const(Q_LORA_RANK, MLA_HEADS * Q_EXT_DIM),
        const(1, KV_LORA_RANK), const(MLA_HEADS * QK_NOPE_DIM, KV_LORA_RANK),
        const(KV_LORA_RANK, MLA_WIDTH),
        pl.BlockSpec((tm, 2 * QK_ROPE_DIM), lambda i: (i % tbl_tiles, 0)),
    ]
    outs = [
        (jax.ShapeDtypeStruct((n, GMLP_WIDTH), BF16), pl.BlockSpec((tm, GMLP_WIDTH), lambda i: (i, 0))),
        (jax.ShapeDtypeStruct((MLA_HEADS, n, QK_DIM), BF16),
         pl.BlockSpec((MLA_HEADS, tm, QK_DIM), lambda i: (0, i, 0))),
        (jax.ShapeDtypeStruct((MLA_HEADS, n_tiles, QK_DIM, tm), BF16),
         pl.BlockSpec((MLA_HEADS, 1, QK_DIM, tm), lambda i: (0, i, 0, 0))),
        (jax.ShapeDtypeStruct((MLA_HEADS, n, V_HEAD_DIM), BF16),
         pl.BlockSpec((MLA_HEADS, tm, V_HEAD_DIM), lambda i: (0, i, 0))),
        (jax.ShapeDtypeStruct((n, KV_LORA_RANK), F32), pl.BlockSpec((tm, KV_LORA_RANK), lambda i: (i, 0))),
        (jax.ShapeDtypeStruct((n, QK_ROPE_DIM), F32), pl.BlockSpec((tm, QK_ROPE_DIM), lambda i: (i, 0))),
    ]
    if emit_vln:
        outs.append((jax.ShapeDtypeStruct((n, GMLP_WIDTH), F32),
                     pl.BlockSpec((tm, GMLP_WIDTH), lambda i: (i, 0))))
    return pl.pallas_call(
        functools.partial(_proj_kernel, tm=tm, seg=seg),
        grid=(n_tiles,),
        in_specs=in_specs,
        out_specs=[spec for _, spec in outs],
        out_shape=[shape for shape, _ in outs],
        compiler_params=pltpu.CompilerParams(
            dimension_semantics=("parallel",), vmem_limit_bytes=V7X_VMEM_LIMIT_BYTES),
        name=f"proj_tm{tm}",
    )(x, w_in_ext, lnv_g, lnv_b, wmix, bmix, g_q, wuq_ext, g_kv, wukt, wuv, rope_t)


def _attn_kernel(q_ref, kt_ref, v_ref, o_ref, s_ref, p_ref, m_ref, l_ref, acc_ref, *, tq, tk, heads):
    qi = pl.program_id(2)
    rb = ATTN_ROW_BLOCK
    m_ref[...] = jnp.full(m_ref.shape, NEG_INF, F32)
    l_ref[...] = jnp.zeros(l_ref.shape, F32)
    acc_ref[...] = jnp.zeros(acc_ref.shape, F32)

    def step(j, diag):
        k0 = pl.multiple_of(j * tk, tk)
        row0 = 0 if diag is None else diag * tk
        live = slice(row0, tq)
        for g in range(heads):
            s_ref[g, live, :] = _dot(q_ref[g, live, :], kt_ref[g, j])
        alphas = []
        for g in range(heads):
            m_old, l_old = m_ref[g, live, :], l_ref[g, live, :]

            def scores(r0):
                s = s_ref[g, r0:r0 + rb, :]
                if diag is not None and (row0 + tk - 1) // CHUNK > r0 // CHUNK:
                    qc = _chunk_of(r0 + lax.broadcasted_iota(jnp.int32, (rb, tk), 0))
                    kc = _chunk_of(row0 + lax.broadcasted_iota(jnp.int32, (rb, tk), 1))
                    s = jnp.where(kc <= qc, s, NEG_INF)
                return s

            blocks = range(row0, tq, rb)
            m_new = jnp.concatenate(
                [jnp.maximum(m_old[r0 - row0:r0 - row0 + rb], jnp.max(scores(r0), axis=-1, keepdims=True))
                 for r0 in blocks], axis=0)
            alpha = jnp.exp2(m_old - m_new)
            sums = []
            for r0 in blocks:
                m_blk = m_new[r0 - row0:r0 - row0 + rb]
                p = jnp.exp2(scores(r0) - jnp.concatenate([m_blk] * (tk // LANES), axis=1))
                p_ref[g, r0:r0 + rb, :] = p.astype(BF16)
                sums.append(jnp.broadcast_to(jnp.sum(p, axis=-1, keepdims=True), (rb, LANES)))
            m_ref[g, live, :] = m_new
            l_ref[g, live, :] = alpha * l_old + jnp.concatenate(sums, axis=0)
            alphas.append(alpha)
        for g in range(heads):
            acc_ref[g, live, :] = (alphas[g] * acc_ref[g, live, :]
                                   + _dot(p_ref[g, live, :], v_ref[g, pl.ds(k0, tk), :]))

    n_full = qi * (tq // tk)

    @pl.loop(0, n_full)
    def _(j):
        step(j, None)

    for d in range(tq // tk):
        step(n_full + d, d)
    for g in range(heads):
        o_ref[:, g * V_HEAD_DIM:(g + 1) * V_HEAD_DIM] = (acc_ref[g] / l_ref[g]).astype(o_ref.dtype)


def _attn_call(q, kt, v, *, batch, seq, tq, tk, heads):
    n = batch * seq
    nq = seq // tq
    assert kt.shape == (MLA_HEADS, n // tk, QK_DIM, tk) and tq % tk == 0
    return pl.pallas_call(
        functools.partial(_attn_kernel, tq=tq, tk=tk, heads=heads),
        grid=(batch, MLA_HEADS // heads, nq),
        in_specs=[
            pl.BlockSpec((heads, tq, QK_DIM), lambda b, h, i: (h, b * nq + i, 0)),
            pl.BlockSpec((heads, seq // tk, QK_DIM, tk), lambda b, h, i: (h, b, 0, 0)),
            pl.BlockSpec((heads, seq, V_HEAD_DIM), lambda b, h, i: (h, b, 0)),
        ],
        out_specs=pl.BlockSpec((tq, heads * V_HEAD_DIM), lambda b, h, i: (b * nq + i, h)),
        out_shape=jax.ShapeDtypeStruct((n, MLA_WIDTH), BF16),
        scratch_shapes=[
            pltpu.VMEM((heads, tq, tk), F32),
            pltpu.VMEM((heads, tq, tk), BF16),
            pltpu.VMEM((heads, tq, LANES), F32),
            pltpu.VMEM((heads, tq, LANES), F32),
            pltpu.VMEM((heads, tq, V_HEAD_DIM), F32),
        ],
        compiler_params=pltpu.CompilerParams(
            dimension_semantics=("parallel", "parallel", "arbitrary"),
            vmem_limit_bytes=V7X_VMEM_LIMIT_BYTES),
        name="attn_prompt",
    )(q, kt, v)


def _attn_cache_kernel(q_ref, cckv_ref, ckpe_ref, nckv_ref, nkpe_ref, wuk_ref, wuv_ref, o_ref,
                       *, sq, past):
    ckv_c = cckv_ref[0].astype(BF16)
    kpe_c = ckpe_ref[0].astype(BF16)
    ckv_n = nckv_ref[...].astype(BF16)
    kpe_n = nkpe_ref[...].astype(BF16)
    q_chunk = _chunk_of(past + lax.broadcasted_iota(jnp.int32, (sq, 1), 0))
    vis_c = _chunk_of(lax.broadcasted_iota(jnp.int32, (sq, past), 1)) <= q_chunk
    vis_n = _chunk_of(past + lax.broadcasted_iota(jnp.int32, (sq, sq), 1)) <= q_chunk
    for h in range(MLA_HEADS):
        qh = q_ref[h]
        wuk_h = wuk_ref[:, h * QK_NOPE_DIM:(h + 1) * QK_NOPE_DIM]
        q_lat = _dot_nt(qh[:, 0:QK_NOPE_DIM], wuk_h).astype(BF16)
        q_pe = qh[:, QK_NOPE_DIM:QK_DIM]
        s_c = jnp.where(vis_c, _dot_nt(q_lat, ckv_c) + _dot_nt(q_pe, kpe_c), NEG_INF)
        s_n = jnp.where(vis_n, _dot_nt(q_lat, ckv_n) + _dot_nt(q_pe, kpe_n), NEG_INF)
        m = jnp.maximum(jnp.max(s_c, axis=-1, keepdims=True), jnp.max(s_n, axis=-1, keepdims=True))
        p_c = jnp.exp2(s_c - m)
        p_n = jnp.exp2(s_n - m)
        l = jnp.sum(p_c, axis=-1, keepdims=True) + jnp.sum(p_n, axis=-1, keepdims=True)
        o_lat = (_dot(p_c.astype(BF16), ckv_c) + _dot(p_n.astype(BF16), ckv_n)) / l
        wuv_h = wuv_ref[:, h * V_HEAD_DIM:(h + 1) * V_HEAD_DIM]
        o_ref[:, h * V_HEAD_DIM:(h + 1) * V_HEAD_DIM] = _dot(o_lat.astype(BF16), wuv_h).astype(o_ref.dtype)


def _attn_cache_call(q, cache_ckv, cache_kpe, new_ckv, new_kpe, wuk, wuv, *, batch, sq):
    past = cache_ckv.shape[1]
    return pl.pallas_call(
        functools.partial(_attn_cache_kernel, sq=sq, past=past),
        grid=(batch,),
        in_specs=[
            pl.BlockSpec((MLA_HEADS, sq, QK_DIM), lambda b: (0, b, 0)),
            pl.BlockSpec((1, past, KV_LORA_RANK), lambda b: (b, 0, 0)),
            pl.BlockSpec((1, past, QK_ROPE_DIM), lambda b: (b, 0, 0)),
            pl.BlockSpec((sq, KV_LORA_RANK), lambda b: (b, 0)),
            pl.BlockSpec((sq, QK_ROPE_DIM), lambda b: (b, 0)),
            pl.BlockSpec(wuk.shape, lambda b: (0, 0)),
            pl.BlockSpec(wuv.shape, lambda b: (0, 0)),
        ],
        out_specs=pl.BlockSpec((sq, MLA_WIDTH), lambda b: (b, 0)),
        out_shape=jax.ShapeDtypeStruct((batch * sq, MLA_WIDTH), BF16),
        compiler_params=pltpu.CompilerParams(
            dimension_semantics=("parallel",), vmem_limit_bytes=V7X_VMEM_LIMIT_BYTES),
        name="attn_cache",
    )(q, cache_ckv, cache_kpe, new_ckv, new_kpe, wuk, wuv)


def _ffn_kernel(x_ref, gm_ref, mla_ref, cpast_ref, wo_ref, ln1g_ref, ln1b_ref, wup_ref, wconv_ref,
                bconv_ref, wdown_ref, ln2g_ref, ln2b_ref, y_ref, cstate_ref, carry_ref, act_ref,
                *, tm, tiles_per_seg, d_ff, cw, alpha):
    i = pl.program_id(0)

    @pl.when(i % tiles_per_seg == 0)
    def _():
        carry_ref[...] = cpast_ref[0]

    mix = _dot(gm_ref[...], wo_ref[0:GMLP_WIDTH, :]) + _dot(mla_ref[...], wo_ref[GMLP_WIDTH:, :])
    h = _layer_norm(alpha * x_ref[...] + mix, ln1g_ref[...], ln1b_ref[...])
    hb = h.astype(BF16)

    row = lax.broadcasted_iota(jnp.int32, (tm, cw), 0)

    def causal_conv(up, off):
        cols = slice(off, off + cw)
        prev2 = carry_ref[0:1, cols]
        prev1 = carry_ref[1:2, cols]
        s1 = jnp.where(row == 0, prev1, pltpu.roll(up, 1, 0))
        s2 = jnp.where(row == 0, prev2, jnp.where(row == 1, prev1, pltpu.roll(up, 2, 0)))
        tail = up[tm - (CONV_W - 1):tm, :]
        carry_ref[:, cols] = tail
        cstate_ref[0, :, cols] = tail
        conv = bconv_ref[:, cols] + wconv_ref[0:1, cols] * s2
        conv = conv + wconv_ref[1:2, cols] * s1
        return conv + wconv_ref[2:3, cols] * up

    for c in range(d_ff // cw):
        a = causal_conv(_dot(hb, wup_ref[:, c * cw:(c + 1) * cw]), c * cw)
        g = causal_conv(_dot(hb, wup_ref[:, d_ff + c * cw:d_ff + (c + 1) * cw]), d_ff + c * cw)
        act_ref[:, c * cw:(c + 1) * cw] = (a * (1.0 / (1.0 + jnp.exp(-a))) * g).astype(BF16)

    ff = _dot(act_ref[...], wdown_ref[...])
    y_ref[...] = _layer_norm(alpha * h + ff, ln2g_ref[...], ln2b_ref[...])


def _ffn_call(x, gm, mla, conv_past, wo, ln1g, ln1b, wup, wconv, bconv, wdown, ln2g, ln2b,
              *, tm, seq, alpha):
    n, d = x.shape
    d_ff = wdown.shape[0]
    batch = n // seq
    tiles_per_seg = seq // tm
    cw = 256
    const = lambda *shape: pl.BlockSpec(shape, lambda i: (0,) * len(shape),
                                        pipeline_mode=pl.Buffered(1))
    return pl.pallas_call(
        functools.partial(_ffn_kernel, tm=tm, tiles_per_seg=tiles_per_seg, d_ff=d_ff, cw=cw,
                          alpha=alpha),
        grid=(n // tm,),
        in_specs=[
            pl.BlockSpec((tm, d), lambda i: (i, 0)),
            pl.BlockSpec((tm, GMLP_WIDTH), lambda i: (i, 0)),
            pl.BlockSpec((tm, MLA_WIDTH), lambda i: (i, 0)),
            pl.BlockSpec((1, CONV_W - 1, 2 * d_ff), lambda i: (i // tiles_per_seg, 0, 0)),
            const(GMLP_WIDTH + MLA_WIDTH, d),
            const(1, d), const(1, d),
            const(d, 2 * d_ff),
            const(CONV_W, 2 * d_ff), const(1, 2 * d_ff),
            const(d_ff, d),
            const(1, d), const(1, d),
        ],
        out_specs=[
            pl.BlockSpec((tm, d), lambda i: (i, 0)),
            pl.BlockSpec((1, CONV_W - 1, 2 * d_ff), lambda i: (i // tiles_per_seg, 0, 0)),
        ],
        out_shape=[
            jax.ShapeDtypeStruct((n, d), F32),
            jax.ShapeDtypeStruct((batch, CONV_W - 1, 2 * d_ff), F32),
        ],
        scratch_shapes=[
            pltpu.VMEM((CONV_W - 1, 2 * d_ff), F32),
            pltpu.VMEM((tm, d_ff), BF16),
        ],
        compiler_params=pltpu.CompilerParams(
            dimension_semantics=("arbitrary",), vmem_limit_bytes=V7X_VMEM_LIMIT_BYTES),
        name=f"ffn_tm{tm}",
    )(x, gm, mla, conv_past, wo, ln1g, ln1b, wup, wconv, bconv, wdown, ln2g, ln2b)


def _rope_tables(pos):
    inv_freq = jnp.power(ROPE_BASE, -jnp.arange(0, QK_ROPE_DIM, 2, dtype=F32) / QK_ROPE_DIM)
    ang = pos.astype(F32)[:, None] * inv_freq[None, :]
    cos, sin = jnp.cos(ang), jnp.sin(ang)
    return jnp.concatenate([cos, cos, -sin, sin], axis=-1)


def _swap_halves(w):
    half = w.shape[-1] // 2
    return jnp.concatenate([w[..., half:], w[..., :half]], axis=-1)


def _layer_weights(l, w_in, ln_v_g, ln_v_b, g_q, w_uq, g_kv, w_uk, w_uv, w_o, ln1_g, ln1_b, w_up,
                   w_conv, b_conv, w_down, ln2_g, ln2_b):
    row = lambda a: a[l].reshape(1, -1)
    o_kpe = 2 * GMLP_WIDTH + Q_LORA_RANK + KV_LORA_RANK
    w_in_ext = jnp.concatenate([w_in[l], _swap_halves(w_in[l][:, o_kpe:])], axis=1).astype(BF16)
    wq = w_uq[l]
    wuq_ext = jnp.concatenate([wq, _swap_halves(wq[..., QK_NOPE_DIM:])], axis=-1)
    wuq_ext = wuq_ext.reshape(Q_LORA_RANK, MLA_HEADS * Q_EXT_DIM).astype(BF16)
    return dict(
        w_in_ext=w_in_ext, lnv_g=row(ln_v_g), lnv_b=row(ln_v_b), g_q=row(g_q), wuq_ext=wuq_ext,
        g_kv=row(g_kv),
        wuk=w_uk[l].reshape(KV_LORA_RANK, MLA_HEADS * QK_NOPE_DIM).astype(BF16),
        wukt=w_uk[l].reshape(KV_LORA_RANK, MLA_HEADS * QK_NOPE_DIM).T.astype(BF16),
        wuv=w_uv[l].reshape(KV_LORA_RANK, MLA_WIDTH).astype(BF16),
        wo=w_o[l].astype(BF16), ln1g=row(ln1_g), ln1b=row(ln1_b), wup=w_up[l].astype(BF16),
        wconv=w_conv[l], bconv=row(b_conv), wdown=w_down[l].astype(BF16), ln2g=row(ln2_g),
        ln2b=row(ln2_b))


def _mix_operands(w_s, b_s, seg):
    reps = GMLP_CHUNK // seg
    wmix = jnp.tile(w_s[:, :seg, :seg], (1, reps, reps))
    bcol = jnp.tile(b_s[:, :seg].T, (reps, 1))
    return wmix, jnp.repeat(bcol, GMLP_HEAD_DIM, axis=1)


def _tiles(batch, seq):
    if seq >= PROMPT_TILE:
        return dict(proj=PROMPT_TILE, attn=ATTN_QUERY_TILE, ffn=PROMPT_TILE)
    assert (batch * seq) % GMLP_CHUNK == 0 and GMLP_CHUNK % seq == 0
    return dict(proj=batch * seq, attn=None, ffn=seq)


def _trunk(x, w, w_s, b_s, *, batch, seq, past, conv_past, alpha, cache=None):
    tiles = _tiles(batch, seq)
    seg = min(seq, GMLP_CHUNK)
    wmix, bmix = _mix_operands(w_s, b_s, seg)
    rope_t = _rope_tables(past + jnp.arange(seq))
    if seq < tiles["proj"]:
        rope_t = jnp.tile(rope_t, (tiles["proj"] // seq, 1))
    gm, q, kt, v, ckv, kpe, *vln = _proj_call(
        x, w["w_in_ext"], w["lnv_g"], w["lnv_b"], wmix, bmix, w["g_q"], w["wuq_ext"], w["g_kv"],
        w["wukt"], w["wuv"], rope_t, tm=tiles["proj"], seg=seg, emit_vln=cache is not None)
    if cache is None:
        mla = _attn_call(q, kt, v, batch=batch, seq=seq, tq=tiles["attn"], tk=tiles["proj"],
                         heads=ATTN_HEADS_PER_STEP)
    else:
        mla = _attn_cache_call(q, cache[0], cache[1], ckv, kpe, w["wuk"], w["wuv"], batch=batch, sq=seq)
    y, conv_state = _ffn_call(x, gm, mla, conv_past, w["wo"], w["ln1g"], w["ln1b"], w["wup"],
                              w["wconv"], w["bconv"], w["wdown"], w["ln2g"], w["ln2b"],
                              tm=tiles["ffn"], seq=seq, alpha=alpha)
    return y, ckv, kpe, (vln[0] if vln else None), conv_state


def kernel(x_prompt, x_sample, cache_ckv, cache_kpe, state_ffn_conv, w_in, ln_v_g, ln_v_b, w_s, b_s, g_q, w_uq, g_kv, w_uk, w_uv, w_o, ln1_g, ln1_b, w_up, w_conv, b_conv, w_down, ln2_g, ln2_b):
    depth = w_in.shape[0]
    alpha = (2.0 * depth) ** 0.25
    bp, sp, d = x_prompt.shape
    bs, ss, _ = x_sample.shape
    d_ff = w_down.shape[1]
    hp = x_prompt.reshape(bp * sp, d)
    hs = x_sample.reshape(bs * ss, d)
    outs = [[] for _ in range(7)]
    for l in range(depth):
        w = _layer_weights(l, w_in, ln_v_g, ln_v_b, g_q, w_uq, g_kv, w_uk, w_uv, w_o, ln1_g, ln1_b,
                           w_up, w_conv, b_conv, w_down, ln2_g, ln2_b)
        hp, ckv_p, kpe_p, _, conv_p = _trunk(
            hp, w, w_s[l], b_s[l], batch=bp, seq=sp, past=0,
            conv_past=jnp.zeros((bp, CONV_W - 1, 2 * d_ff), F32), alpha=alpha)
        hs, ckv_s, kpe_s, v_s, conv_s = _trunk(
            hs, w, w_s[l], b_s[l], batch=bs, seq=ss, past=cache_ckv.shape[2],
            conv_past=state_ffn_conv[l], alpha=alpha, cache=(cache_ckv[l], cache_kpe[l]))
        for acc, val in zip(outs, (
                ckv_p.reshape(bp, sp, -1), kpe_p.reshape(bp, sp, -1), conv_p,
                ckv_s.reshape(bs, ss, -1), kpe_s.reshape(bs, ss, -1), v_s.reshape(bs, ss, -1), conv_s)):
            acc.append(val)
    return (hp.reshape(bp, sp, d), hs.reshape(bs, ss, d), *(jnp.stack(o) for o in outs))
```

```python
import functools
import math

import jax
import jax.numpy as jnp
from jax import lax
from jax.experimental import pallas as pl
from jax.experimental.pallas import tpu as pltpu

CHUNK = 64
GMLP_CHUNK = 128
GMLP_HEADS = 4
GMLP_HEAD_DIM = 128
GMLP_WIDTH = GMLP_HEADS * GMLP_HEAD_DIM
MLA_HEADS = 4
QK_NOPE_DIM = 128
QK_ROPE_DIM = 64
QK_DIM = QK_NOPE_DIM + QK_ROPE_DIM
V_HEAD_DIM = 128
MLA_WIDTH = MLA_HEADS * V_HEAD_DIM
Q_LORA_RANK = 384
KV_LORA_RANK = 256
CONV_W = 3
ROPE_BASE = 10000.0
ATTN_SCALE = 1.0 / math.sqrt(QK_NOPE_DIM + QK_ROPE_DIM)
Q_SCALE = ATTN_SCALE * math.log2(math.e)
LN_EPS = 1e-5
RMS_EPS = 1e-6
NEG_INF = -1e30

Q_EXT_DIM = QK_NOPE_DIM + 2 * QK_ROPE_DIM

V7X_VMEM_LIMIT_BYTES = 56 * 1024 * 1024
LANES = 128
FFN_COLUMN_CHUNK = 256
assert V_HEAD_DIM == LANES
PROMPT_TILE = 512
ATTN_QUERY_TILE = 1024
ATTN_HEADS_PER_STEP = 2
ATTN_ROW_BLOCK = 32

F32 = jnp.float32
BF16 = jnp.bfloat16


def _layer_norm(x, g, b):
    mu = jnp.mean(x, axis=-1, keepdims=True)
    xc = x - mu
    var = jnp.mean(xc * xc, axis=-1, keepdims=True)
    return xc * lax.rsqrt(var + LN_EPS) * g + b


def _rms_norm(x, g):
    return x * lax.rsqrt(jnp.mean(x * x, axis=-1, keepdims=True) + RMS_EPS) * g


def _chunk_of(pos):
    return jnp.right_shift(pos, CHUNK.bit_length() - 1)


def _rope(pair, table):
    t = pair * table
    return t + pltpu.roll(t, QK_ROPE_DIM, 1)


def _dot(a, b):
    return jnp.dot(a, b, preferred_element_type=F32)


def _dot_nt(a, b):
    return lax.dot_general(a, b, (((1,), (1,)), ((), ())), preferred_element_type=F32)


def _proj_kernel(x_ref, w_in_ref, lnv_g_ref, lnv_b_ref, wmix_ref, bmix_ref, gq_ref, wuq_ref,
                 gkv_ref, wukt_ref, wuv_ref, rope_ref,
                 gm_ref, q_ref, kt_ref, v_ref, ckv_ref, kpe_ref, *maybe_vln_ref, tm, seg):
    x = x_ref[...].astype(BF16)
    proj = _dot(x, w_in_ref[...])
    o_v, o_cq = GMLP_WIDTH, 2 * GMLP_WIDTH
    o_ckv = o_cq + Q_LORA_RANK
    o_kpe = o_ckv + KV_LORA_RANK

    u = jax.nn.gelu(proj[:, 0:o_v])
    v = _layer_norm(jax.nn.gelu(proj[:, o_v:o_cq]), lnv_g_ref[...], lnv_b_ref[...])
    for vln_ref in maybe_vln_ref:
        vln_ref[...] = v
    vb = v.astype(BF16)
    row = lax.broadcasted_iota(jnp.int32, (GMLP_CHUNK, GMLP_CHUNK), 0)
    col = lax.broadcasted_iota(jnp.int32, (GMLP_CHUNK, GMLP_CHUNK), 1)
    same_chunk = jnp.bitwise_xor(row, col) < seg
    keep = same_chunk & (_chunk_of(jnp.bitwise_and(col, seg - 1)) <= _chunk_of(jnp.bitwise_and(row, seg - 1)))
    for h in range(GMLP_HEADS):
        cs = slice(h * GMLP_HEAD_DIM, (h + 1) * GMLP_HEAD_DIM)
        w_h = jnp.where(keep, wmix_ref[h], 0.0).astype(BF16)
        chunks = [slice(c * GMLP_CHUNK, (c + 1) * GMLP_CHUNK) for c in range(tm // GMLP_CHUNK)]
        mixed = _dot(w_h, jnp.concatenate([vb[rs, cs] for rs in chunks], axis=1))
        for c, rs in enumerate(chunks):
            s = mixed[:, c * GMLP_HEAD_DIM:(c + 1) * GMLP_HEAD_DIM] + bmix_ref[:, cs]
            gm_ref[rs, cs] = (u[rs, cs] * s).astype(BF16)

    rope = rope_ref[...]

    cqn = _rms_norm(proj[:, o_cq:o_ckv], gq_ref[...]).astype(BF16)
    q = _dot(cqn, wuq_ref[...])
    for h in range(MLA_HEADS):
        b0 = h * Q_EXT_DIM
        q_pe = _rope(q[:, b0 + QK_NOPE_DIM:b0 + Q_EXT_DIM], rope)[:, 0:QK_ROPE_DIM]
        q_ref[h, :, 0:QK_NOPE_DIM] = (q[:, b0:b0 + QK_NOPE_DIM] * Q_SCALE).astype(BF16)
        q_ref[h, :, QK_NOPE_DIM:QK_DIM] = (q_pe * Q_SCALE).astype(BF16)

    ckv = _rms_norm(proj[:, o_ckv:o_kpe], gkv_ref[...])
    ckv_ref[...] = ckv
    kpe2 = _rope(proj[:, o_kpe:o_kpe + 2 * QK_ROPE_DIM], rope)
    kpe_ref[...] = kpe2[:, 0:QK_ROPE_DIM]
    cb = ckv.astype(BF16)
    k_nope_t = _dot_nt(wukt_ref[...], cb)
    kpe_t = kpe2.T[0:QK_ROPE_DIM, :].astype(BF16)
    v_up = _dot(cb, wuv_ref[...])
    for h in range(MLA_HEADS):
        kt_ref[h, 0, 0:QK_NOPE_DIM, :] = k_nope_t[h * QK_NOPE_DIM:(h + 1) * QK_NOPE_DIM, :].astype(BF16)
        kt_ref[h, 0, QK_NOPE_DIM:QK_DIM, :] = kpe_t
        v_ref[h] = v_up[:, h * V_HEAD_DIM:(h + 1) * V_HEAD_DIM].astype(BF16)


def _proj_call(x, w_in_ext, lnv_g, lnv_b, wmix, bmix, g_q, wuq_ext, g_kv, wukt, wuv, rope_t,
               *, tm, seg, emit_vln):
    n, d = x.shape
    in_w = w_in_ext.shape[1]
    n_tiles = n // tm
    tbl_tiles = rope_t.shape[0] // tm
    const = lambda *shape: pl.BlockSpec(shape, lambda i: (0,) * len(shape))
    in_specs = [
        pl.BlockSpec((tm, d), lambda i: (i, 0)),
        const(d, in_w),
        const(1, GMLP_WIDTH), const(1, GMLP_WIDTH),
        const(GMLP_HEADS, GMLP_CHUNK, GMLP_CHUNK), const(GMLP_CHUNK, GMLP_WIDTH),
        const(1, Q_LORA_RANK), const(Q_LORA_RANK, MLA_HEADS * Q_EXT_DIM),
        const(1, KV_LORA_RANK), const(MLA_HEADS * QK_NOPE_DIM, KV_LORA_RANK),
        const(KV_LORA_RANK, MLA_WIDTH),
        pl.BlockSpec((tm, 2 * QK_ROPE_DIM), lambda i: (i % tbl_tiles, 0)),
    ]
    outs = [
        (jax.ShapeDtypeStruct((n, GMLP_WIDTH), BF16), pl.BlockSpec((tm, GMLP_WIDTH), lambda i: (i, 0))),
        (jax.ShapeDtypeStruct((MLA_HEADS, n, QK_DIM), BF16),
         pl.BlockSpec((MLA_HEADS, tm, QK_DIM), lambda i: (0, i, 0))),
        (jax.ShapeDtypeStruct((MLA_HEADS, n_tiles, QK_DIM, tm), BF16),
         pl.BlockSpec((MLA_HEADS, 1, QK_DIM, tm), lambda i: (0, i, 0, 0))),
        (jax.ShapeDtypeStruct((MLA_HEADS, n, V_HEAD_DIM), BF16),
         pl.BlockSpec((MLA_HEADS, tm, V_HEAD_DIM), lambda i: (0, i, 0))),
        (jax.ShapeDtypeStruct((n, KV_LORA_RANK), F32), pl.BlockSpec((tm, KV_LORA_RANK), lambda i: (i, 0))),
        (jax.ShapeDtypeStruct((n, QK_ROPE_DIM), F32), pl.BlockSpec((tm, QK_ROPE_DIM), lambda i: (i, 0))),
    ]
    if emit_vln:
        outs.append((jax.ShapeDtypeStruct((n, GMLP_WIDTH), F32),
                     pl.BlockSpec((tm, GMLP_WIDTH), lambda i: (i, 0))))
    return pl.pallas_call(
        functools.partial(_proj_kernel, tm=tm, seg=seg),
        grid=(n_tiles,),
        in_specs=in_specs,
        out_specs=[spec for _, spec in outs],
        out_shape=[shape for shape, _ in outs],
        compiler_params=pltpu.CompilerParams(
            dimension_semantics=("parallel",), vmem_limit_bytes=V7X_VMEM_LIMIT_BYTES),
        name=f"proj_tm{tm}",
    )(x, w_in_ext, lnv_g, lnv_b, wmix, bmix, g_q, wuq_ext, g_kv, wukt, wuv, rope_t)


def _attn_kernel(q_ref, kt_ref, v_ref, o_ref, *scratch, tq, tk, heads):
    qi = pl.program_id(2)
    rb = ATTN_ROW_BLOCK
    group = tq // tk
    per_head = [scratch[4 * g:4 * g + 4] for g in range(heads)]
    for _, _, m_ref, acc_ref in per_head:
        m_ref[...] = jnp.full(m_ref.shape, NEG_INF, F32)
        acc_ref[...] = jnp.zeros(acc_ref.shape, F32)

    def update(r_lo, r_hi, j0, n_tiles, last_is_diagonal):
        width = n_tiles * tk
        k0 = pl.multiple_of(j0 * tk, tk)
        ones = jnp.ones((width, LANES), BF16)
        blocks = range(r_lo, r_hi, rb)

        def score_matmul(g):
            s_ref, _, m_ref, _ = per_head[g]
            lane_max = None
            for t in range(n_tiles):
                s = _dot(q_ref[g, r_lo:r_hi, :], kt_ref[g, j0 + t])
                if last_is_diagonal and t == n_tiles - 1:
                    qc = _chunk_of(lax.broadcasted_iota(jnp.int32, s.shape, 0))
                    kc = _chunk_of(lax.broadcasted_iota(jnp.int32, s.shape, 1))
                    s = jnp.where(kc <= qc, s, NEG_INF)
                s_ref[r_lo:r_hi, t * tk:(t + 1) * tk] = s
                for c in range(tk // LANES):
                    part = s[:, c * LANES:(c + 1) * LANES]
                    lane_max = part if lane_max is None else jnp.maximum(lane_max, part)
            m_old = m_ref[r_lo:r_hi, :]
            m_new = jnp.maximum(m_old, jnp.max(lane_max, axis=-1, keepdims=True))
            m_ref[r_lo:r_hi, :] = m_new
            return m_old, m_new

        def exponentials(g, m_new):
            s_ref, p_ref, _, _ = per_head[g]
            for r0 in blocks:
                m_blk = m_new[r0 - r_lo:r0 - r_lo + rb]
                p = jnp.exp2(s_ref[r0:r0 + rb, 0:width] - jnp.concatenate([m_blk] * (width // LANES), axis=1))
                p_ref[r0:r0 + rb, 0:width] = p.astype(BF16)

        def value_matmul(g, m_old, m_new):
            _, p_ref, _, acc_ref = per_head[g]
            alpha = jnp.exp2(m_old - m_new)
            v_ones = jnp.concatenate([v_ref[g, pl.ds(k0, width), :], ones], axis=1)
            acc_ref[r_lo:r_hi, :] = (jnp.concatenate([alpha] * 2, axis=1) * acc_ref[r_lo:r_hi, :]
                                     + _dot(p_ref[r_lo:r_hi, 0:width], v_ones))

        maxima = [score_matmul(g) for g in range(heads)]
        for g in range(heads):
            exponentials(g, maxima[g][1])
        for g in range(heads):
            value_matmul(g, *maxima[g])

    @pl.loop(0, qi)
    def _(jj):
        update(0, tq, jj * group, group, False)

    for d in range(group):
        update(d * tk, (d + 1) * tk, qi * group, d + 1, True)
    for g, (_, _, _, acc_ref) in enumerate(per_head):
        o_ref[:, g * V_HEAD_DIM:(g + 1) * V_HEAD_DIM] = (
            acc_ref[:, 0:V_HEAD_DIM] / acc_ref[:, V_HEAD_DIM:]).astype(o_ref.dtype)


def _attn_call(q, kt, v, *, batch, seq, tq, tk, heads):
    n = batch * seq
    nq = seq // tq
    assert kt.shape == (MLA_HEADS, n // tk, QK_DIM, tk) and tq % tk == 0
    return pl.pallas_call(
        functools.partial(_attn_kernel, tq=tq, tk=tk, heads=heads),
        grid=(batch, MLA_HEADS // heads, nq),
        in_specs=[
            pl.BlockSpec((heads, tq, QK_DIM), lambda b, h, i: (h, b * nq + i, 0)),
            pl.BlockSpec((heads, seq // tk, QK_DIM, tk), lambda b, h, i: (h, b, 0, 0)),
            pl.BlockSpec((heads, seq, V_HEAD_DIM), lambda b, h, i: (h, b, 0)),
        ],
        out_specs=pl.BlockSpec((tq, heads * V_HEAD_DIM), lambda b, h, i: (b * nq + i, h)),
        out_shape=jax.ShapeDtypeStruct((n, MLA_WIDTH), BF16),
        scratch_shapes=[
            pltpu.VMEM((tq, tq), F32),
            pltpu.VMEM((tq, tq), BF16),
            pltpu.VMEM((tq, LANES), F32),
            pltpu.VMEM((tq, V_HEAD_DIM + LANES), F32),
        ] * heads,
        compiler_params=pltpu.CompilerParams(
            dimension_semantics=("parallel", "parallel", "arbitrary"),
            vmem_limit_bytes=V7X_VMEM_LIMIT_BYTES),
        name="attn_prompt",
    )(q, kt, v)


def _attn_cache_kernel(q_ref, cckv_ref, ckpe_ref, nckv_ref, nkpe_ref, wuk_ref, wuv_ref, o_ref,
                       *, sq, past):
    ckv_c = cckv_ref[0].astype(BF16)
    kpe_c = ckpe_ref[0].astype(BF16)
    ckv_n = nckv_ref[...].astype(BF16)
    kpe_n = nkpe_ref[...].astype(BF16)
    rows = MLA_HEADS * sq
    q_pos = jnp.concatenate([past + lax.broadcasted_iota(jnp.int32, (sq, 1), 0)] * MLA_HEADS, axis=0)
    vis_c = _chunk_of(lax.broadcasted_iota(jnp.int32, (rows, past), 1)) <= _chunk_of(q_pos)
    vis_n = _chunk_of(past + lax.broadcasted_iota(jnp.int32, (rows, sq), 1)) <= _chunk_of(q_pos)
    q_lat = jnp.concatenate(
        [_dot_nt(q_ref[h][:, 0:QK_NOPE_DIM], wuk_ref[:, h * QK_NOPE_DIM:(h + 1) * QK_NOPE_DIM])
         for h in range(MLA_HEADS)], axis=0).astype(BF16)
    q_pe = jnp.concatenate([q_ref[h][:, QK_NOPE_DIM:QK_DIM] for h in range(MLA_HEADS)], axis=0)
    s_c = jnp.where(vis_c, _dot_nt(q_lat, ckv_c) + _dot_nt(q_pe, kpe_c), NEG_INF)
    s_n = jnp.where(vis_n, _dot_nt(q_lat, ckv_n) + _dot_nt(q_pe, kpe_n), NEG_INF)
    m = jnp.maximum(jnp.max(s_c, axis=-1, keepdims=True), jnp.max(s_n, axis=-1, keepdims=True))
    p_c = jnp.exp2(s_c - m)
    p_n = jnp.exp2(s_n - m)
    l = jnp.sum(p_c, axis=-1, keepdims=True) + jnp.sum(p_n, axis=-1, keepdims=True)
    o_lat = ((_dot(p_c.astype(BF16), ckv_c) + _dot(p_n.astype(BF16), ckv_n)) / l).astype(BF16)
    for h in range(MLA_HEADS):
        wuv_h = wuv_ref[:, h * V_HEAD_DIM:(h + 1) * V_HEAD_DIM]
        o_ref[:, h * V_HEAD_DIM:(h + 1) * V_HEAD_DIM] = _dot(o_lat[h * sq:(h + 1) * sq], wuv_h).astype(o_ref.dtype)


def _attn_cache_call(q, cache_ckv, cache_kpe, new_ckv, new_kpe, wuk, wuv, *, batch, sq):
    past = cache_ckv.shape[1]
    return pl.pallas_call(
        functools.partial(_attn_cache_kernel, sq=sq, past=past),
        grid=(batch,),
        in_specs=[
            pl.BlockSpec((MLA_HEADS, sq, QK_DIM), lambda b: (0, b, 0)),
            pl.BlockSpec((1, past, KV_LORA_RANK), lambda b: (b, 0, 0)),
            pl.BlockSpec((1, past, QK_ROPE_DIM), lambda b: (b, 0, 0)),
            pl.BlockSpec((sq, KV_LORA_RANK), lambda b: (b, 0)),
            pl.BlockSpec((sq, QK_ROPE_DIM), lambda b: (b, 0)),
            pl.BlockSpec(wuk.shape, lambda b: (0, 0)),
            pl.BlockSpec(wuv.shape, lambda b: (0, 0)),
        ],
        out_specs=pl.BlockSpec((sq, MLA_WIDTH), lambda b: (b, 0)),
        out_shape=jax.ShapeDtypeStruct((batch * sq, MLA_WIDTH), BF16),
        compiler_params=pltpu.CompilerParams(
            dimension_semantics=("parallel",), vmem_limit_bytes=V7X_VMEM_LIMIT_BYTES),
        name="attn_cache",
    )(q, cache_ckv, cache_kpe, new_ckv, new_kpe, wuk, wuv)


def _ffn_kernel(x_ref, gm_ref, mla_ref, cpast_ref, wo_ref, ln1g_ref, ln1b_ref, wup_ref, wconv_ref,
                bconv_ref, wdown_ref, ln2g_ref, ln2b_ref, y_ref, cstate_ref, carry_ref, act_ref,
                *, tm, seq_rows, tiles_per_seg, d_ff, cw, alpha):
    i = pl.program_id(0)
    n_seq = tm // seq_rows

    if n_seq == 1:
        @pl.when(i % tiles_per_seg == 0)
        def _():
            carry_ref[...] = cpast_ref[0]

    mix = _dot(gm_ref[...], wo_ref[0:GMLP_WIDTH, :]) + _dot(mla_ref[...], wo_ref[GMLP_WIDTH:, :])
    h = _layer_norm(alpha * x_ref[...] + mix, ln1g_ref[...], ln1b_ref[...])
    hb = h.astype(BF16)

    row = lax.broadcasted_iota(jnp.int32, (tm, cw), 0)

    def causal_conv(up, off):
        cols = slice(off, off + cw)
        s1, s2 = pltpu.roll(up, 1, 0), pltpu.roll(up, 2, 0)
        for k in range(n_seq):
            first = k * seq_rows
            history = carry_ref if n_seq == 1 else cpast_ref.at[k]
            prev2 = history[0:1, cols]
            prev1 = history[1:2, cols]
            s1 = jnp.where(row == first, prev1, s1)
            s2 = jnp.where(row == first, prev2, jnp.where(row == first + 1, prev1, s2))
            tail = up[first + seq_rows - (CONV_W - 1):first + seq_rows, :]
            if n_seq == 1:
                carry_ref[:, cols] = tail
            cstate_ref[k, :, cols] = tail
        conv = bconv_ref[:, cols] + wconv_ref[0:1, cols] * s2
        conv = conv + wconv_ref[1:2, cols] * s1
        return conv + wconv_ref[2:3, cols] * up

    for c in range(d_ff // cw):
        a = causal_conv(_dot(hb, wup_ref[:, c * cw:(c + 1) * cw]), c * cw)
        g = causal_conv(_dot(hb, wup_ref[:, d_ff + c * cw:d_ff + (c + 1) * cw]), d_ff + c * cw)
        act_ref[:, c * cw:(c + 1) * cw] = (a * (1.0 / (1.0 + jnp.exp(-a))) * g).astype(BF16)

    ff = _dot(act_ref[...], wdown_ref[...])
    y_ref[...] = _layer_norm(alpha * h + ff, ln2g_ref[...], ln2b_ref[...])


def _ffn_call(x, gm, mla, conv_past, wo, ln1g, ln1b, wup, wconv, bconv, wdown, ln2g, ln2b,
              *, tm, seq, alpha):
    n, d = x.shape
    d_ff = wdown.shape[0]
    batch = n // seq
    seq_rows = min(seq, tm)
    n_seq = tm // seq_rows
    tiles_per_seg = seq // seq_rows
    cw = FFN_COLUMN_CHUNK
    const = lambda *shape: pl.BlockSpec(shape, lambda i: (0,) * len(shape),
                                        pipeline_mode=pl.Buffered(1))
    return pl.pallas_call(
        functools.partial(_ffn_kernel, tm=tm, seq_rows=seq_rows, tiles_per_seg=tiles_per_seg, d_ff=d_ff,
                          cw=cw, alpha=alpha),
        grid=(n // tm,),
        in_specs=[
            pl.BlockSpec((tm, d), lambda i: (i, 0)),
            pl.BlockSpec((tm, GMLP_WIDTH), lambda i: (i, 0)),
            pl.BlockSpec((tm, MLA_WIDTH), lambda i: (i, 0)),
            pl.BlockSpec((n_seq, CONV_W - 1, 2 * d_ff), lambda i: (i // tiles_per_seg, 0, 0)),
            const(GMLP_WIDTH + MLA_WIDTH, d),
            const(1, d), const(1, d),
            const(d, 2 * d_ff),
            const(CONV_W, 2 * d_ff), const(1, 2 * d_ff),
            const(d_ff, d),
            const(1, d), const(1, d),
        ],
        out_specs=[
            pl.BlockSpec((tm, d), lambda i: (i, 0)),
            pl.BlockSpec((n_seq, CONV_W - 1, 2 * d_ff), lambda i: (i // tiles_per_seg, 0, 0)),
        ],
        out_shape=[
            jax.ShapeDtypeStruct((n, d), F32),
            jax.ShapeDtypeStruct((batch, CONV_W - 1, 2 * d_ff), F32),
        ],
        scratch_shapes=[
            pltpu.VMEM((CONV_W - 1, 2 * d_ff), F32),
            pltpu.VMEM((tm, d_ff), BF16),
        ],
        compiler_params=pltpu.CompilerParams(
            dimension_semantics=("arbitrary",), vmem_limit_bytes=V7X_VMEM_LIMIT_BYTES),
        name=f"ffn_tm{tm}",
    )(x, gm, mla, conv_past, wo, ln1g, ln1b, wup, wconv, bconv, wdown, ln2g, ln2b)


def _rope_tables(first_pos, n_pos, reps):
    inv_freq = jnp.power(ROPE_BASE, -jnp.arange(0, QK_ROPE_DIM, 2, dtype=F32) / QK_ROPE_DIM)
    ang = (first_pos + jnp.arange(n_pos)).astype(F32)[:, None] * inv_freq[None, :]
    cos, sin = jnp.cos(ang), jnp.sin(ang)
    return jnp.tile(jnp.concatenate([cos, cos, -sin, sin], axis=-1), (reps, 1))


def _swap_halves(w):
    half = w.shape[-1] // 2
    return jnp.concatenate([w[..., half:], w[..., :half]], axis=-1)


def _layer_weights(l, w_in, ln_v_g, ln_v_b, g_q, w_uq, g_kv, w_uk, w_uv, w_o, ln1_g, ln1_b, w_up,
                   w_conv, b_conv, w_down, ln2_g, ln2_b):
    row = lambda a: a[l].reshape(1, -1)
    o_kpe = 2 * GMLP_WIDTH + Q_LORA_RANK + KV_LORA_RANK
    w_in_ext = jnp.concatenate([w_in[l], _swap_halves(w_in[l][:, o_kpe:])], axis=1).astype(BF16)
    wq = w_uq[l]
    wuq_ext = jnp.concatenate([wq, _swap_halves(wq[..., QK_NOPE_DIM:])], axis=-1)
    wuq_ext = wuq_ext.reshape(Q_LORA_RANK, MLA_HEADS * Q_EXT_DIM).astype(BF16)
    return dict(
        w_in_ext=w_in_ext, lnv_g=row(ln_v_g), lnv_b=row(ln_v_b), g_q=row(g_q), wuq_ext=wuq_ext,
        g_kv=row(g_kv),
        wuk=w_uk[l].reshape(KV_LORA_RANK, MLA_HEADS * QK_NOPE_DIM).astype(BF16),
        wukt=w_uk[l].reshape(KV_LORA_RANK, MLA_HEADS * QK_NOPE_DIM).T.astype(BF16),
        wuv=w_uv[l].reshape(KV_LORA_RANK, MLA_WIDTH).astype(BF16),
        wo=w_o[l].astype(BF16), ln1g=row(ln1_g), ln1b=row(ln1_b), wup=w_up[l].astype(BF16),
        wconv=w_conv[l], bconv=row(b_conv), wdown=w_down[l].astype(BF16), ln2g=row(ln2_g),
        ln2b=row(ln2_b))


def _mix_operands(w_s, b_s, seg):
    reps = GMLP_CHUNK // seg
    wmix = jnp.tile(w_s[:, :seg, :seg], (1, reps, reps))
    bcol = jnp.tile(b_s[:, :seg].T, (reps, 1))
    return wmix, jnp.repeat(bcol, GMLP_HEAD_DIM, axis=1)


def _tiles(batch, seq):
    if seq >= PROMPT_TILE:
        return dict(proj=PROMPT_TILE, attn=ATTN_QUERY_TILE, ffn=PROMPT_TILE)
    assert (batch * seq) % GMLP_CHUNK == 0 and GMLP_CHUNK % seq == 0
    return dict(proj=batch * seq, attn=None, ffn=batch * seq)


def _trunk(x, w, w_s, b_s, *, batch, seq, past, conv_past, alpha, cache=None):
    tiles = _tiles(batch, seq)
    seg = min(seq, GMLP_CHUNK)
    wmix, bmix = _mix_operands(w_s, b_s, seg)
    rope_t = _rope_tables(past, seq, max(1, tiles["proj"] // seq))
    gm, q, kt, v, ckv, kpe, *vln = _proj_call(
        x, w["w_in_ext"], w["lnv_g"], w["lnv_b"], wmix, bmix, w["g_q"], w["wuq_ext"], w["g_kv"],
        w["wukt"], w["wuv"], rope_t, tm=tiles["proj"], seg=seg, emit_vln=cache is not None)
    if cache is None:
        mla = _attn_call(q, kt, v, batch=batch, seq=seq, tq=tiles["attn"], tk=tiles["proj"],
                         heads=ATTN_HEADS_PER_STEP)
    else:
        mla = _attn_cache_call(q, cache[0], cache[1], ckv, kpe, w["wuk"], w["wuv"], batch=batch, sq=seq)
    y, conv_state = _ffn_call(x, gm, mla, conv_past, w["wo"], w["ln1g"], w["ln1b"], w["wup"],
                              w["wconv"], w["bconv"], w["wdown"], w["ln2g"], w["ln2b"],
                              tm=tiles["ffn"], seq=seq, alpha=alpha)
    return y, ckv, kpe, (vln[0] if vln else None), conv_state


def kernel(x_prompt, x_sample, cache_ckv, cache_kpe, state_ffn_conv, w_in, ln_v_g, ln_v_b, w_s, b_s, g_q, w_uq, g_kv, w_uk, w_uv, w_o, ln1_g, ln1_b, w_up, w_conv, b_conv, w_down, ln2_g, ln2_b):
    depth = w_in.shape[0]
    alpha = (2.0 * depth) ** 0.25
    bp, sp, d = x_prompt.shape
    bs, ss, _ = x_sample.shape
    d_ff = w_down.shape[1]
    hp = x_prompt.reshape(bp * sp, d)
    hs = x_sample.reshape(bs * ss, d)
    outs = [[] for _ in range(7)]
    for l in range(depth):
        w = _layer_weights(l, w_in, ln_v_g, ln_v_b, g_q, w_uq, g_kv, w_uk, w_uv, w_o, ln1_g, ln1_b,
                           w_up, w_conv, b_conv, w_down, ln2_g, ln2_b)
        hp, ckv_p, kpe_p, _, conv_p = _trunk(
            hp, w, w_s[l], b_s[l], batch=bp, seq=sp, past=0,
            conv_past=jnp.zeros((bp, CONV_W - 1, 2 * d_ff), F32), alpha=alpha)
        hs, ckv_s, kpe_s, v_s, conv_s = _trunk(
            hs, w, w_s[l], b_s[l], batch=bs, seq=ss, past=cache_ckv.shape[2],
            conv_past=state_ffn_conv[l], alpha=alpha, cache=(cache_ckv[l], cache_kpe[l]))
        for acc, val in zip(outs, (
                ckv_p.reshape(bp, sp, -1), kpe_p.reshape(bp, sp, -1), conv_p,
                ckv_s.reshape(bs, ss, -1), kpe_s.reshape(bs, ss, -1), v_s.reshape(bs, ss, -1), conv_s)):
            acc.append(val)
    return (hp.reshape(bp, sp, d), hs.reshape(bs, ss, d), *(jnp.stack(o) for o in outs))
```

```python
import functools
import math

import jax
import jax.numpy as jnp
from jax import lax
from jax.experimental import pallas as pl
from jax.experimental.pallas import tpu as pltpu

CHUNK = 64
GMLP_CHUNK = 128
GMLP_HEADS = 4
GMLP_HEAD_DIM = 128
GMLP_WIDTH = GMLP_HEADS * GMLP_HEAD_DIM
MLA_HEADS = 4
QK_NOPE_DIM = 128
QK_ROPE_DIM = 64
QK_DIM = QK_NOPE_DIM + QK_ROPE_DIM
V_HEAD_DIM = 128
MLA_WIDTH = MLA_HEADS * V_HEAD_DIM
Q_LORA_RANK = 384
KV_LORA_RANK = 256
CONV_W = 3
ROPE_BASE = 10000.0
ATTN_SCALE = 1.0 / math.sqrt(QK_NOPE_DIM + QK_ROPE_DIM)
Q_SCALE = ATTN_SCALE * math.log2(math.e)
LN_EPS = 1e-5
RMS_EPS = 1e-6
NEG_INF = -1e30

Q_EXT_DIM = QK_NOPE_DIM + 2 * QK_ROPE_DIM

V7X_VMEM_LIMIT_BYTES = 56 * 1024 * 1024
LANES = 128
FFN_COLUMN_CHUNK = 256
assert V_HEAD_DIM == LANES
PROMPT_TILE = 512
ATTN_QUERY_TILE = 1024
ATTN_HEADS_PER_STEP = 2
ATTN_ROW_BLOCK = 32

F32 = jnp.float32
BF16 = jnp.bfloat16


def _layer_norm(x, g, b):
    mu = jnp.mean(x, axis=-1, keepdims=True)
    xc = x - mu
    var = jnp.mean(xc * xc, axis=-1, keepdims=True)
    return xc * lax.rsqrt(var + LN_EPS) * g + b


def _rms_norm(x, g):
    return x * lax.rsqrt(jnp.mean(x * x, axis=-1, keepdims=True) + RMS_EPS) * g


def _chunk_of(pos):
    return jnp.right_shift(pos, CHUNK.bit_length() - 1)


def _rope(pair, table):
    t = pair * table
    return t + pltpu.roll(t, QK_ROPE_DIM, 1)


def _dot(a, b):
    return jnp.dot(a, b, preferred_element_type=F32)


def _dot_nt(a, b):
    return lax.dot_general(a, b, (((1,), (1,)), ((), ())), preferred_element_type=F32)


def _proj_kernel(x_ref, w_in_ref, lnv_g_ref, lnv_b_ref, wmix_ref, bmix_ref, gq_ref, wuq_ref,
                 gkv_ref, wukt_ref, wuv_ref, rope_ref,
                 gm_ref, q_ref, kt_ref, v_ref, ckv_ref, kpe_ref, *maybe_vln_ref, tm, seg):
    x = x_ref[...].astype(BF16)
    proj = _dot(x, w_in_ref[...])
    o_v, o_cq = GMLP_WIDTH, 2 * GMLP_WIDTH
    o_ckv = o_cq + Q_LORA_RANK
    o_kpe = o_ckv + KV_LORA_RANK

    u = jax.nn.gelu(proj[:, 0:o_v])
    v = _layer_norm(jax.nn.gelu(proj[:, o_v:o_cq]), lnv_g_ref[...], lnv_b_ref[...])
    for vln_ref in maybe_vln_ref:
        vln_ref[...] = v
    vb = v.astype(BF16)
    row = lax.broadcasted_iota(jnp.int32, (GMLP_CHUNK, GMLP_CHUNK), 0)
    col = lax.broadcasted_iota(jnp.int32, (GMLP_CHUNK, GMLP_CHUNK), 1)
    same_chunk = jnp.bitwise_xor(row, col) < seg
    keep = same_chunk & (_chunk_of(jnp.bitwise_and(col, seg - 1)) <= _chunk_of(jnp.bitwise_and(row, seg - 1)))
    for h in range(GMLP_HEADS):
        cs = slice(h * GMLP_HEAD_DIM, (h + 1) * GMLP_HEAD_DIM)
        w_h = jnp.where(keep, wmix_ref[h], 0.0).astype(BF16)
        chunks = [slice(c * GMLP_CHUNK, (c + 1) * GMLP_CHUNK) for c in range(tm // GMLP_CHUNK)]
        mixed = _dot(w_h, jnp.concatenate([vb[rs, cs] for rs in chunks], axis=1))
        for c, rs in enumerate(chunks):
            s = mixed[:, c * GMLP_HEAD_DIM:(c + 1) * GMLP_HEAD_DIM] + bmix_ref[:, cs]
            gm_ref[rs, cs] = (u[rs, cs] * s).astype(BF16)

    rope = rope_ref[...]

    cqn = _rms_norm(proj[:, o_cq:o_ckv], gq_ref[...]).astype(BF16)
    q = _dot(cqn, wuq_ref[...])
    for h in range(MLA_HEADS):
        b0 = h * Q_EXT_DIM
        q_pe = _rope(q[:, b0 + QK_NOPE_DIM:b0 + Q_EXT_DIM], rope)[:, 0:QK_ROPE_DIM]
        q_ref[h, :, 0:QK_NOPE_DIM] = (q[:, b0:b0 + QK_NOPE_DIM] * Q_SCALE).astype(BF16)
        q_ref[h, :, QK_NOPE_DIM:QK_DIM] = (q_pe * Q_SCALE).astype(BF16)

    ckv = _rms_norm(proj[:, o_ckv:o_kpe], gkv_ref[...])
    ckv_ref[...] = ckv
    kpe2 = _rope(proj[:, o_kpe:o_kpe + 2 * QK_ROPE_DIM], rope)
    kpe_ref[...] = kpe2[:, 0:QK_ROPE_DIM]
    cb = ckv.astype(BF16)
    k_nope_t = _dot_nt(wukt_ref[...], cb)
    kpe_t = kpe2.T[0:QK_ROPE_DIM, :].astype(BF16)
    v_up = _dot(cb, wuv_ref[...])
    for h in range(MLA_HEADS):
        kt_ref[h, 0, 0:QK_NOPE_DIM, :] = k_nope_t[h * QK_NOPE_DIM:(h + 1) * QK_NOPE_DIM, :].astype(BF16)
        kt_ref[h, 0, QK_NOPE_DIM:QK_DIM, :] = kpe_t
        v_ref[h] = v_up[:, h * V_HEAD_DIM:(h + 1) * V_HEAD_DIM].astype(BF16)


def _proj_call(x, w_in_ext, lnv_g, lnv_b, wmix, bmix, g_q, wuq_ext, g_kv, wukt, wuv, rope_t,
               *, tm, seg, emit_vln):
    n, d = x.shape
    in_w = w_in_ext.shape[1]
    n_tiles = n // tm
    tbl_tiles = rope_t.shape[0] // tm
    const = lambda *shape: pl.BlockSpec(shape, lambda i: (0,) * len(shape))
    in_specs = [
        pl.BlockSpec((tm, d), lambda i: (i, 0)),
        const(d, in_w),
        const(1, GMLP_WIDTH), const(1, GMLP_WIDTH),
        const(GMLP_HEADS, GMLP_CHUNK, GMLP_CHUNK), const(GMLP_CHUNK, GMLP_WIDTH),
        const(1, Q_LORA_RANK), const(Q_LORA_RANK, MLA_HEADS * Q_EXT_DIM),
        const(1, KV_LORA_RANK), const(MLA_HEADS * QK_NOPE_DIM, KV_LORA_RANK),
        const(KV_LORA_RANK, MLA_WIDTH),
        pl.BlockSpec((tm, 2 * QK_ROPE_DIM), lambda i: (i % tbl_tiles, 0)),
    ]
    outs = [
        (jax.ShapeDtypeStruct((n, GMLP_WIDTH), BF16), pl.BlockSpec((tm, GMLP_WIDTH), lambda i: (i, 0))),
        (jax.ShapeDtypeStruct((MLA_HEADS, n, QK_DIM), BF16),
         pl.BlockSpec((MLA_HEADS, tm, QK_DIM), lambda i: (0, i, 0))),
        (jax.ShapeDtypeStruct((MLA_HEADS, n_tiles, QK_DIM, tm), BF16),
         pl.BlockSpec((MLA_HEADS, 1, QK_DIM, tm), lambda i: (0, i, 0, 0))),
        (jax.ShapeDtypeStruct((MLA_HEADS, n, V_HEAD_DIM), BF16),
         pl.BlockSpec((MLA_HEADS, tm, V_HEAD_DIM), lambda i: (0, i, 0))),
        (jax.ShapeDtypeStruct((n, KV_LORA_RANK), F32), pl.BlockSpec((tm, KV_LORA_RANK), lambda i: (i, 0))),
        (jax.ShapeDtypeStruct((n, QK_ROPE_DIM), F32), pl.BlockSpec((tm, QK_ROPE_DIM), lambda i: (i, 0))),
    ]
    if emit_vln:
        outs.append((jax.ShapeDtypeStruct((n, GMLP_WIDTH), F32),
                     pl.BlockSpec((tm, GMLP_WIDTH), lambda i: (i, 0))))
    return pl.pallas_call(
        functools.partial(_proj_kernel, tm=tm, seg=seg),
        grid=(n_tiles,),
        in_specs=in_specs,
        out_specs=[spec for _, spec in outs],
        out_shape=[shape for shape, _ in outs],
        compiler_params=pltpu.CompilerParams(
            dimension_semantics=("parallel",), vmem_limit_bytes=V7X_VMEM_LIMIT_BYTES),
        name=f"proj_tm{tm}",
    )(x, w_in_ext, lnv_g, lnv_b, wmix, bmix, g_q, wuq_ext, g_kv, wukt, wuv, rope_t)


def _attn_kernel(q_ref, kt_ref, v_ref, o_ref, *scratch, tq, tk, heads):
    qi = pl.program_id(2)
    rb = ATTN_ROW_BLOCK
    group = tq // tk
    n_buf = len(scratch) // heads
    per_head = [scratch[n_buf * g:n_buf * (g + 1)] for g in range(heads)]
    for _, _, _, m_ref, acc_ref in per_head:
        m_ref[...] = jnp.full(m_ref.shape, NEG_INF, F32)
        acc_ref[...] = jnp.zeros(acc_ref.shape, F32)

    def score_stage(g, j0):
        s_ref, _, t_ref, _, _ = per_head[g]
        lane_max = None
        for t in range(group):
            s = _dot(q_ref[g], kt_ref[g, j0 * group + t])
            s_ref[:, t * tk:(t + 1) * tk] = s
            for c in range(tk // LANES):
                part = s[:, c * LANES:(c + 1) * LANES]
                lane_max = part if lane_max is None else jnp.maximum(lane_max, part)
        t_ref[...] = jnp.broadcast_to(jnp.max(lane_max, axis=-1, keepdims=True), lane_max.shape)

    def softmax_stage(g, j0, own_keys):
        s_ref, p_ref, t_ref, m_ref, acc_ref = per_head[g]
        blocks = range(0, tq, rb)

        def width(r0):
            return (r0 // tk + 1) * tk if own_keys else tq

        def scores(r0):
            s = s_ref[r0:r0 + rb, 0:width(r0)]
            if own_keys and (tk - 1) // CHUNK > (r0 % tk) // CHUNK:
                qc = _chunk_of(r0 % tk + lax.broadcasted_iota(jnp.int32, (rb, tk), 0))
                kc = _chunk_of(lax.broadcasted_iota(jnp.int32, (rb, tk), 1))
                visible = [s[:, 0:width(r0) - tk]] if width(r0) > tk else []
                s = jnp.concatenate(visible + [jnp.where(kc <= qc, s[:, width(r0) - tk:], NEG_INF)], axis=1)
            return s

        if own_keys:
            t_max = jnp.concatenate(
                [jnp.broadcast_to(jnp.max(scores(r0), axis=-1, keepdims=True), (rb, LANES)) for r0 in blocks],
                axis=0)
        else:
            t_max = t_ref[...]
        m_old = m_ref[...]
        m_new = jnp.maximum(m_old, t_max)
        m_ref[...] = m_new
        for r0 in blocks:
            p = jnp.exp2(scores(r0) - jnp.concatenate([m_new[r0:r0 + rb]] * (width(r0) // LANES), axis=1))
            p_ref[r0:r0 + rb, 0:width(r0)] = p.astype(BF16)
        alpha = jnp.concatenate([jnp.exp2(m_old - m_new)] * 2, axis=1)
        k0 = pl.multiple_of(j0 * tq, tq)
        v_ones = jnp.concatenate([v_ref[g, pl.ds(k0, tq), :], jnp.ones((tq, LANES), BF16)], axis=1)
        for r_lo in range(0, tq, tk) if own_keys else (0,):
            r_hi = r_lo + tk if own_keys else tq
            acc_ref[r_lo:r_hi, :] = (alpha[r_lo:r_hi] * acc_ref[r_lo:r_hi, :]
                                     + _dot(p_ref[r_lo:r_hi, 0:width(r_lo)], v_ones[0:width(r_lo)]))

    score_stage(0, 0)

    @pl.loop(0, qi)
    def _(jj):
        for g in range(1, heads):
            score_stage(g, jj)
        softmax_stage(0, jj, False)
        score_stage(0, jj + 1)
        for g in range(1, heads):
            softmax_stage(g, jj, False)

    for g in range(1, heads):
        score_stage(g, qi)
    for g in range(heads):
        softmax_stage(g, qi, True)
    for g, (_, _, _, _, acc_ref) in enumerate(per_head):
        o_ref[:, g * V_HEAD_DIM:(g + 1) * V_HEAD_DIM] = (
            acc_ref[:, 0:V_HEAD_DIM] / acc_ref[:, V_HEAD_DIM:]).astype(o_ref.dtype)


def _attn_call(q, kt, v, *, batch, seq, tq, tk, heads):
    n = batch * seq
    nq = seq // tq
    assert kt.shape == (MLA_HEADS, n // tk, QK_DIM, tk) and tq % tk == 0
    return pl.pallas_call(
        functools.partial(_attn_kernel, tq=tq, tk=tk, heads=heads),
        grid=(batch, MLA_HEADS // heads, nq),
        in_specs=[
            pl.BlockSpec((heads, tq, QK_DIM), lambda b, h, i: (h, b * nq + i, 0)),
            pl.BlockSpec((heads, seq // tk, QK_DIM, tk), lambda b, h, i: (h, b, 0, 0)),
            pl.BlockSpec((heads, seq, V_HEAD_DIM), lambda b, h, i: (h, b, 0)),
        ],
        out_specs=pl.BlockSpec((tq, heads * V_HEAD_DIM), lambda b, h, i: (b * nq + i, h)),
        out_shape=jax.ShapeDtypeStruct((n, MLA_WIDTH), BF16),
        scratch_shapes=[
            pltpu.VMEM((tq, tq), F32),
            pltpu.VMEM((tq, tq), BF16),
            pltpu.VMEM((tq, LANES), F32),
            pltpu.VMEM((tq, LANES), F32),
            pltpu.VMEM((tq, V_HEAD_DIM + LANES), F32),
        ] * heads,
        compiler_params=pltpu.CompilerParams(
            dimension_semantics=("parallel", "parallel", "arbitrary"),
            vmem_limit_bytes=V7X_VMEM_LIMIT_BYTES),
        name="attn_prompt",
    )(q, kt, v)


def _attn_cache_kernel(q_ref, cckv_ref, ckpe_ref, nckv_ref, nkpe_ref, wuk_ref, wuv_ref, o_ref,
                       *, sq, past):
    ckv_c = cckv_ref[0].astype(BF16)
    kpe_c = ckpe_ref[0].astype(BF16)
    ckv_n = nckv_ref[...].astype(BF16)
    kpe_n = nkpe_ref[...].astype(BF16)
    rows = MLA_HEADS * sq
    q_pos = jnp.concatenate([past + lax.broadcasted_iota(jnp.int32, (sq, 1), 0)] * MLA_HEADS, axis=0)
    vis_c = _chunk_of(lax.broadcasted_iota(jnp.int32, (rows, past), 1)) <= _chunk_of(q_pos)
    vis_n = _chunk_of(past + lax.broadcasted_iota(jnp.int32, (rows, sq), 1)) <= _chunk_of(q_pos)
    q_lat = jnp.concatenate(
        [_dot_nt(q_ref[h][:, 0:QK_NOPE_DIM], wuk_ref[:, h * QK_NOPE_DIM:(h + 1) * QK_NOPE_DIM])
         for h in range(MLA_HEADS)], axis=0).astype(BF16)
    q_pe = jnp.concatenate([q_ref[h][:, QK_NOPE_DIM:QK_DIM] for h in range(MLA_HEADS)], axis=0)
    s_c = jnp.where(vis_c, _dot_nt(q_lat, ckv_c) + _dot_nt(q_pe, kpe_c), NEG_INF)
    s_n = jnp.where(vis_n, _dot_nt(q_lat, ckv_n) + _dot_nt(q_pe, kpe_n), NEG_INF)
    m = jnp.maximum(jnp.max(s_c, axis=-1, keepdims=True), jnp.max(s_n, axis=-1, keepdims=True))
    p_c = jnp.exp2(s_c - m)
    p_n = jnp.exp2(s_n - m)
    l = jnp.sum(p_c, axis=-1, keepdims=True) + jnp.sum(p_n, axis=-1, keepdims=True)
    o_lat = ((_dot(p_c.astype(BF16), ckv_c) + _dot(p_n.astype(BF16), ckv_n)) / l).astype(BF16)
    for h in range(MLA_HEADS):
        wuv_h = wuv_ref[:, h * V_HEAD_DIM:(h + 1) * V_HEAD_DIM]
        o_ref[:, h * V_HEAD_DIM:(h + 1) * V_HEAD_DIM] = _dot(o_lat[h * sq:(h + 1) * sq], wuv_h).astype(o_ref.dtype)


def _attn_cache_call(q, cache_ckv, cache_kpe, new_ckv, new_kpe, wuk, wuv, *, batch, sq):
    past = cache_ckv.shape[1]
    return pl.pallas_call(
        functools.partial(_attn_cache_kernel, sq=sq, past=past),
        grid=(batch,),
        in_specs=[
            pl.BlockSpec((MLA_HEADS, sq, QK_DIM), lambda b: (0, b, 0)),
            pl.BlockSpec((1, past, KV_LORA_RANK), lambda b: (b, 0, 0)),
            pl.BlockSpec((1, past, QK_ROPE_DIM), lambda b: (b, 0, 0)),
            pl.BlockSpec((sq, KV_LORA_RANK), lambda b: (b, 0)),
            pl.BlockSpec((sq, QK_ROPE_DIM), lambda b: (b, 0)),
            pl.BlockSpec(wuk.shape, lambda b: (0, 0)),
            pl.BlockSpec(wuv.shape, lambda b: (0, 0)),
        ],
        out_specs=pl.BlockSpec((sq, MLA_WIDTH), lambda b: (b, 0)),
        out_shape=jax.ShapeDtypeStruct((batch * sq, MLA_WIDTH), BF16),
        compiler_params=pltpu.CompilerParams(
            dimension_semantics=("parallel",), vmem_limit_bytes=V7X_VMEM_LIMIT_BYTES),
        name="attn_cache",
    )(q, cache_ckv, cache_kpe, new_ckv, new_kpe, wuk, wuv)


def _ffn_kernel(x_ref, gm_ref, mla_ref, cpast_ref, wo_ref, ln1g_ref, ln1b_ref, wup_ref, wconv_ref,
                bconv_ref, wdown_ref, ln2g_ref, ln2b_ref, y_ref, cstate_ref, carry_ref, act_ref,
                *, tm, seq_rows, tiles_per_seg, d_ff, cw, alpha):
    i = pl.program_id(0)
    n_seq = tm // seq_rows

    if n_seq == 1:
        @pl.when(i % tiles_per_seg == 0)
        def _():
            carry_ref[...] = cpast_ref[0]

    mix = _dot(gm_ref[...], wo_ref[0:GMLP_WIDTH, :]) + _dot(mla_ref[...], wo_ref[GMLP_WIDTH:, :])
    h = _layer_norm(alpha * x_ref[...] + mix, ln1g_ref[...], ln1b_ref[...])
    hb = h.astype(BF16)

    row = lax.broadcasted_iota(jnp.int32, (tm, cw), 0)

    def causal_conv(up, off):
        cols = slice(off, off + cw)
        s1, s2 = pltpu.roll(up, 1, 0), pltpu.roll(up, 2, 0)
        for k in range(n_seq):
            first = k * seq_rows
            history = carry_ref if n_seq == 1 else cpast_ref.at[k]
            prev2 = history[0:1, cols]
            prev1 = history[1:2, cols]
            s1 = jnp.where(row == first, prev1, s1)
            s2 = jnp.where(row == first, prev2, jnp.where(row == first + 1, prev1, s2))
            tail = up[first + seq_rows - (CONV_W - 1):first + seq_rows, :]
            if n_seq == 1:
                carry_ref[:, cols] = tail
            cstate_ref[k, :, cols] = tail
        conv = bconv_ref[:, cols] + wconv_ref[0:1, cols] * s2
        conv = conv + wconv_ref[1:2, cols] * s1
        return conv + wconv_ref[2:3, cols] * up

    for c in range(d_ff // cw):
        a = causal_conv(_dot(hb, wup_ref[:, c * cw:(c + 1) * cw]), c * cw)
        g = causal_conv(_dot(hb, wup_ref[:, d_ff + c * cw:d_ff + (c + 1) * cw]), d_ff + c * cw)
        act_ref[:, c * cw:(c + 1) * cw] = (a * (1.0 / (1.0 + jnp.exp(-a))) * g).astype(BF16)

    ff = _dot(act_ref[...], wdown_ref[...])
    y_ref[...] = _layer_norm(alpha * h + ff, ln2g_ref[...], ln2b_ref[...])


def _ffn_call(x, gm, mla, conv_past, wo, ln1g, ln1b, wup, wconv, bconv, wdown, ln2g, ln2b,
              *, tm, seq, alpha):
    n, d = x.shape
    d_ff = wdown.shape[0]
    batch = n // seq
    seq_rows = min(seq, tm)
    n_seq = tm // seq_rows
    tiles_per_seg = seq // seq_rows
    cw = FFN_COLUMN_CHUNK
    const = lambda *shape: pl.BlockSpec(shape, lambda i: (0,) * len(shape),
                                        pipeline_mode=pl.Buffered(1))
    return pl.pallas_call(
        functools.partial(_ffn_kernel, tm=tm, seq_rows=seq_rows, tiles_per_seg=tiles_per_seg, d_ff=d_ff,
                          cw=cw, alpha=alpha),
        grid=(n // tm,),
        in_specs=[
            pl.BlockSpec((tm, d), lambda i: (i, 0)),
            pl.BlockSpec((tm, GMLP_WIDTH), lambda i: (i, 0)),
            pl.BlockSpec((tm, MLA_WIDTH), lambda i: (i, 0)),
            pl.BlockSpec((n_seq, CONV_W - 1, 2 * d_ff), lambda i: (i // tiles_per_seg, 0, 0)),
            const(GMLP_WIDTH + MLA_WIDTH, d),
            const(1, d), const(1, d),
            const(d, 2 * d_ff),
            const(CONV_W, 2 * d_ff), const(1, 2 * d_ff),
            const(d_ff, d),
            const(1, d), const(1, d),
        ],
        out_specs=[
            pl.BlockSpec((tm, d), lambda i: (i, 0)),
            pl.BlockSpec((n_seq, CONV_W - 1, 2 * d_ff), lambda i: (i // tiles_per_seg, 0, 0)),
        ],
        out_shape=[
            jax.ShapeDtypeStruct((n, d), F32),
            jax.ShapeDtypeStruct((batch, CONV_W - 1, 2 * d_ff), F32),
        ],
        scratch_shapes=[
            pltpu.VMEM((CONV_W - 1, 2 * d_ff), F32),
            pltpu.VMEM((tm, d_ff), BF16),
        ],
        compiler_params=pltpu.CompilerParams(
            dimension_semantics=("arbitrary",), vmem_limit_bytes=V7X_VMEM_LIMIT_BYTES),
        name=f"ffn_tm{tm}",
    )(x, gm, mla, conv_past, wo, ln1g, ln1b, wup, wconv, bconv, wdown, ln2g, ln2b)


def _rope_tables(first_pos, n_pos, reps):
    inv_freq = jnp.power(ROPE_BASE, -jnp.arange(0, QK_ROPE_DIM, 2, dtype=F32) / QK_ROPE_DIM)
    ang = (first_pos + jnp.arange(n_pos)).astype(F32)[:, None] * inv_freq[None, :]
    cos, sin = jnp.cos(ang), jnp.sin(ang)
    return jnp.tile(jnp.concatenate([cos, cos, -sin, sin], axis=-1), (reps, 1))


def _swap_halves(w):
    half = w.shape[-1] // 2
    return jnp.concatenate([w[..., half:], w[..., :half]], axis=-1)


def _layer_weights(l, w_in, ln_v_g, ln_v_b, g_q, w_uq, g_kv, w_uk, w_uv, w_o, ln1_g, ln1_b, w_up,
                   w_conv, b_conv, w_down, ln2_g, ln2_b):
    row = lambda a: a[l].reshape(1, -1)
    o_kpe = 2 * GMLP_WIDTH + Q_LORA_RANK + KV_LORA_RANK
    w_in_ext = jnp.concatenate([w_in[l], _swap_halves(w_in[l][:, o_kpe:])], axis=1).astype(BF16)
    wq = w_uq[l]
    wuq_ext = jnp.concatenate([wq, _swap_halves(wq[..., QK_NOPE_DIM:])], axis=-1)
    wuq_ext = wuq_ext.reshape(Q_LORA_RANK, MLA_HEADS * Q_EXT_DIM).astype(BF16)
    return dict(
        w_in_ext=w_in_ext, lnv_g=row(ln_v_g), lnv_b=row(ln_v_b), g_q=row(g_q), wuq_ext=wuq_ext,
        g_kv=row(g_kv),
        wuk=w_uk[l].reshape(KV_LORA_RANK, MLA_HEADS * QK_NOPE_DIM).astype(BF16),
        wukt=w_uk[l].reshape(KV_LORA_RANK, MLA_HEADS * QK_NOPE_DIM).T.astype(BF16),
        wuv=w_uv[l].reshape(KV_LORA_RANK, MLA_WIDTH).astype(BF16),
        wo=w_o[l].astype(BF16), ln1g=row(ln1_g), ln1b=row(ln1_b), wup=w_up[l].astype(BF16),
        wconv=w_conv[l], bconv=row(b_conv), wdown=w_down[l].astype(BF16), ln2g=row(ln2_g),
        ln2b=row(ln2_b))


def _mix_operands(w_s, b_s, seg):
    reps = GMLP_CHUNK // seg
    wmix = jnp.tile(w_s[:, :seg, :seg], (1, reps, reps))
    bcol = jnp.tile(b_s[:, :seg].T, (reps, 1))
    return wmix, jnp.repeat(bcol, GMLP_HEAD_DIM, axis=1)


def _tiles(batch, seq):
    if seq >= PROMPT_TILE:
        return dict(proj=PROMPT_TILE, attn=ATTN_QUERY_TILE, ffn=PROMPT_TILE)
    assert (batch * seq) % GMLP_CHUNK == 0 and GMLP_CHUNK % seq == 0
    return dict(proj=batch * seq, attn=None, ffn=batch * seq)


def _trunk(x, w, w_s, b_s, *, batch, seq, past, conv_past, alpha, cache=None):
    tiles = _tiles(batch, seq)
    seg = min(seq, GMLP_CHUNK)
    wmix, bmix = _mix_operands(w_s, b_s, seg)
    rope_t = _rope_tables(past, seq, max(1, tiles["proj"] // seq))
    gm, q, kt, v, ckv, kpe, *vln = _proj_call(
        x, w["w_in_ext"], w["lnv_g"], w["lnv_b"], wmix, bmix, w["g_q"], w["wuq_ext"], w["g_kv"],
        w["wukt"], w["wuv"], rope_t, tm=tiles["proj"], seg=seg, emit_vln=cache is not None)
    if cache is None:
        mla = _attn_call(q, kt, v, batch=batch, seq=seq, tq=tiles["attn"], tk=tiles["proj"],
                         heads=ATTN_HEADS_PER_STEP)
    else:
        mla = _attn_cache_call(q, cache[0], cache[1], ckv, kpe, w["wuk"], w["wuv"], batch=batch, sq=seq)
    y, conv_state = _ffn_call(x, gm, mla, conv_past, w["wo"], w["ln1g"], w["ln1b"], w["wup"],
                              w["wconv"], w["bconv"], w["wdown"], w["ln2g"], w["ln2b"],
                              tm=tiles["ffn"], seq=seq, alpha=alpha)
    return y, ckv, kpe, (vln[0] if vln else None), conv_state


def kernel(x_prompt, x_sample, cache_ckv, cache_kpe, state_ffn_conv, w_in, ln_v_g, ln_v_b, w_s, b_s, g_q, w_uq, g_kv, w_uk, w_uv, w_o, ln1_g, ln1_b, w_up, w_conv, b_conv, w_down, ln2_g, ln2_b):
    depth = w_in.shape[0]
    alpha = (2.0 * depth) ** 0.25
    bp, sp, d = x_prompt.shape
    bs, ss, _ = x_sample.shape
    d_ff = w_down.shape[1]
    hp = x_prompt.reshape(bp * sp, d)
    hs = x_sample.reshape(bs * ss, d)
    outs = [[] for _ in range(7)]
    for l in range(depth):
        w = _layer_weights(l, w_in, ln_v_g, ln_v_b, g_q, w_uq, g_kv, w_uk, w_uv, w_o, ln1_g, ln1_b,
                           w_up, w_conv, b_conv, w_down, ln2_g, ln2_b)
        hp, ckv_p, kpe_p, _, conv_p = _trunk(
            hp, w, w_s[l], b_s[l], batch=bp, seq=sp, past=0,
            conv_past=jnp.zeros((bp, CONV_W - 1, 2 * d_ff), F32), alpha=alpha)
        hs, ckv_s, kpe_s, v_s, conv_s = _trunk(
            hs, w, w_s[l], b_s[l], batch=bs, seq=ss, past=cache_ckv.shape[2],
            conv_past=state_ffn_conv[l], alpha=alpha, cache=(cache_ckv[l], cache_kpe[l]))
        for acc, val in zip(outs, (
                ckv_p.reshape(bp, sp, -1), kpe_p.reshape(bp, sp, -1), conv_p,
                ckv_s.reshape(bs, ss, -1), kpe_s.reshape(bs, ss, -1), v_s.reshape(bs, ss, -1), conv_s)):
            acc.append(val)
    return (hp.reshape(bp, sp, d), hs.reshape(bs, ss, d), *(jnp.stack(o) for o in outs))
```

```python
import functools
import math

import jax
import jax.numpy as jnp
from jax import lax
from jax.experimental import pallas as pl
from jax.experimental.pallas import tpu as pltpu

CHUNK = 64
GMLP_CHUNK = 128
GMLP_HEADS = 4
GMLP_HEAD_DIM = 128
GMLP_WIDTH = GMLP_HEADS * GMLP_HEAD_DIM
MLA_HEADS = 4
QK_NOPE_DIM = 128
QK_ROPE_DIM = 64
QK_DIM = QK_NOPE_DIM + QK_ROPE_DIM
V_HEAD_DIM = 128
MLA_WIDTH = MLA_HEADS * V_HEAD_DIM
Q_LORA_RANK = 384
KV_LORA_RANK = 256
CONV_W = 3
ROPE_BASE = 10000.0
ATTN_SCALE = 1.0 / math.sqrt(QK_NOPE_DIM + QK_ROPE_DIM)
Q_SCALE = ATTN_SCALE * math.log2(math.e)
LN_EPS = 1e-5
RMS_EPS = 1e-6
NEG_INF = -1e30

Q_EXT_DIM = QK_NOPE_DIM + 2 * QK_ROPE_DIM

V7X_VMEM_LIMIT_BYTES = 56 * 1024 * 1024
LANES = 128
FFN_COLUMN_CHUNK = 256
assert V_HEAD_DIM == LANES
PROMPT_TILE = 512
ATTN_QUERY_TILE = 1024
ATTN_HEADS_PER_STEP = 2
ATTN_ROW_BLOCK = 32

F32 = jnp.float32
BF16 = jnp.bfloat16


def _layer_norm(x, g, b):
    mu = jnp.mean(x, axis=-1, keepdims=True)
    xc = x - mu
    var = jnp.mean(xc * xc, axis=-1, keepdims=True)
    return xc * lax.rsqrt(var + LN_EPS) * g + b


def _rms_norm(x, g):
    return x * lax.rsqrt(jnp.mean(x * x, axis=-1, keepdims=True) + RMS_EPS) * g


def _chunk_of(pos):
    return jnp.right_shift(pos, CHUNK.bit_length() - 1)


def _rope(pair, table):
    t = pair * table
    return t + pltpu.roll(t, QK_ROPE_DIM, 1)


def _dot(a, b):
    return jnp.dot(a, b, preferred_element_type=F32)


def _dot_nt(a, b):
    return lax.dot_general(a, b, (((1,), (1,)), ((), ())), preferred_element_type=F32)


def _proj_kernel(x_ref, w_in_ref, lnv_g_ref, lnv_b_ref, wmix_ref, bmix_ref, gq_ref, wuq_ref,
                 gkv_ref, wukt_ref, wuv_ref, rope_ref,
                 gm_ref, q_ref, kt_ref, v_ref, ckv_ref, kpe_ref, *maybe_vln_ref, tm, seg):
    x = x_ref[...].astype(BF16)
    proj = _dot(x, w_in_ref[...])
    o_v, o_cq = GMLP_WIDTH, 2 * GMLP_WIDTH
    o_ckv = o_cq + Q_LORA_RANK
    o_kpe = o_ckv + KV_LORA_RANK

    u = jax.nn.gelu(proj[:, 0:o_v])
    v = _layer_norm(jax.nn.gelu(proj[:, o_v:o_cq]), lnv_g_ref[...], lnv_b_ref[...])
    for vln_ref in maybe_vln_ref:
        vln_ref[...] = v
    vb = v.astype(BF16)
    row = lax.broadcasted_iota(jnp.int32, (GMLP_CHUNK, GMLP_CHUNK), 0)
    col = lax.broadcasted_iota(jnp.int32, (GMLP_CHUNK, GMLP_CHUNK), 1)
    same_chunk = jnp.bitwise_xor(row, col) < seg
    keep = same_chunk & (_chunk_of(jnp.bitwise_and(col, seg - 1)) <= _chunk_of(jnp.bitwise_and(row, seg - 1)))
    for h in range(GMLP_HEADS):
        cs = slice(h * GMLP_HEAD_DIM, (h + 1) * GMLP_HEAD_DIM)
        w_h = jnp.where(keep, wmix_ref[h], 0.0).astype(BF16)
        chunks = [slice(c * GMLP_CHUNK, (c + 1) * GMLP_CHUNK) for c in range(tm // GMLP_CHUNK)]
        mixed = _dot(w_h, jnp.concatenate([vb[rs, cs] for rs in chunks], axis=1))
        for c, rs in enumerate(chunks):
            s = mixed[:, c * GMLP_HEAD_DIM:(c + 1) * GMLP_HEAD_DIM] + bmix_ref[:, cs]
            gm_ref[rs, cs] = (u[rs, cs] * s).astype(BF16)

    rope = rope_ref[...]

    cqn = _rms_norm(proj[:, o_cq:o_ckv], gq_ref[...]).astype(BF16)
    q = _dot(cqn, wuq_ref[...])
    for h in range(MLA_HEADS):
        b0 = h * Q_EXT_DIM
        q_pe = _rope(q[:, b0 + QK_NOPE_DIM:b0 + Q_EXT_DIM], rope)[:, 0:QK_ROPE_DIM]
        q_ref[h, :, 0:QK_NOPE_DIM] = (q[:, b0:b0 + QK_NOPE_DIM] * Q_SCALE).astype(BF16)
        q_ref[h, :, QK_NOPE_DIM:QK_DIM] = (q_pe * Q_SCALE).astype(BF16)

    ckv = _rms_norm(proj[:, o_ckv:o_kpe], gkv_ref[...])
    ckv_ref[...] = ckv
    kpe2 = _rope(proj[:, o_kpe:o_kpe + 2 * QK_ROPE_DIM], rope)
    kpe_ref[...] = kpe2[:, 0:QK_ROPE_DIM]
    cb = ckv.astype(BF16)
    k_nope_t = _dot_nt(wukt_ref[...], cb)
    kpe_t = kpe2.T[0:QK_ROPE_DIM, :].astype(BF16)
    v_up = _dot(cb, wuv_ref[...])
    for h in range(MLA_HEADS):
        kt_ref[h, 0, 0:QK_NOPE_DIM, :] = k_nope_t[h * QK_NOPE_DIM:(h + 1) * QK_NOPE_DIM, :].astype(BF16)
        kt_ref[h, 0, QK_NOPE_DIM:QK_DIM, :] = kpe_t
        v_ref[h] = v_up[:, h * V_HEAD_DIM:(h + 1) * V_HEAD_DIM].astype(BF16)


def _proj_call(x, w_in_ext, lnv_g, lnv_b, wmix, bmix, g_q, wuq_ext, g_kv, wukt, wuv, rope_t,
               *, tm, seg, emit_vln):
    n, d = x.shape
    in_w = w_in_ext.shape[1]
    n_tiles = n // tm
    tbl_tiles = rope_t.shape[0] // tm
    const = lambda *shape: pl.BlockSpec(shape, lambda i: (0,) * len(shape))
    in_specs = [
        pl.BlockSpec((tm, d), lambda i: (i, 0)),
        const(d, in_w),
        const(1, GMLP_WIDTH), const(1, GMLP_WIDTH),
        const(GMLP_HEADS, GMLP_CHUNK, GMLP_CHUNK), const(GMLP_CHUNK, GMLP_WIDTH),
        const(1, Q_LORA_RANK), const(Q_LORA_RANK, MLA_HEADS * Q_EXT_DIM),
        const(1, KV_LORA_RANK), const(MLA_HEADS * QK_NOPE_DIM, KV_LORA_RANK),
        const(KV_LORA_RANK, MLA_WIDTH),
        pl.BlockSpec((tm, 2 * QK_ROPE_DIM), lambda i: (i % tbl_tiles, 0)),
    ]
    outs = [
        (jax.ShapeDtypeStruct((n, GMLP_WIDTH), BF16), pl.BlockSpec((tm, GMLP_WIDTH), lambda i: (i, 0))),
        (jax.ShapeDtypeStruct((MLA_HEADS, n, QK_DIM), BF16),
         pl.BlockSpec((MLA_HEADS, tm, QK_DIM), lambda i: (0, i, 0))),
        (jax.ShapeDtypeStruct((MLA_HEADS, n_tiles, QK_DIM, tm), BF16),
         pl.BlockSpec((MLA_HEADS, 1, QK_DIM, tm), lambda i: (0, i, 0, 0))),
        (jax.ShapeDtypeStruct((MLA_HEADS, n, V_HEAD_DIM), BF16),
         pl.BlockSpec((MLA_HEADS, tm, V_HEAD_DIM), lambda i: (0, i, 0))),
        (jax.ShapeDtypeStruct((n, KV_LORA_RANK), F32), pl.BlockSpec((tm, KV_LORA_RANK), lambda i: (i, 0))),
        (jax.ShapeDtypeStruct((n, QK_ROPE_DIM), F32), pl.BlockSpec((tm, QK_ROPE_DIM), lambda i: (i, 0))),
    ]
    if emit_vln:
        outs.append((jax.ShapeDtypeStruct((n, GMLP_WIDTH), F32),
                     pl.BlockSpec((tm, GMLP_WIDTH), lambda i: (i, 0))))
    return pl.pallas_call(
        functools.partial(_proj_kernel, tm=tm, seg=seg),
        grid=(n_tiles,),
        in_specs=in_specs,
        out_specs=[spec for _, spec in outs],
        out_shape=[shape for shape, _ in outs],
        compiler_params=pltpu.CompilerParams(
            dimension_semantics=("parallel",), vmem_limit_bytes=V7X_VMEM_LIMIT_BYTES),
        name=f"proj_tm{tm}",
    )(x, w_in_ext, lnv_g, lnv_b, wmix, bmix, g_q, wuq_ext, g_kv, wukt, wuv, rope_t)


def _attn_kernel(q_ref, kt_ref, v_ref, o_ref, *scratch, tq, tk, heads):
    qi = pl.program_id(2)
    rb = ATTN_ROW_BLOCK
    group = tq // tk
    per_head = [scratch[3 * g:3 * g + 3] for g in range(heads)]
    for _, m_ref, acc_ref in per_head:
        m_ref[...] = jnp.full(m_ref.shape, NEG_INF, F32)
        acc_ref[...] = jnp.zeros(acc_ref.shape, F32)

    def update(r_lo, r_hi, j0, n_tiles, last_is_diagonal):
        width = n_tiles * tk
        k0 = pl.multiple_of(j0 * tk, tk)
        ones = jnp.ones((width, LANES), BF16)
        blocks = range(r_lo, r_hi, rb)

        def score_matmul(g):
            s_ref, m_ref, _ = per_head[g]
            lane_max = None
            for t in range(n_tiles):
                s = _dot(q_ref[g, r_lo:r_hi, :], kt_ref[g, j0 + t])
                if last_is_diagonal and t == n_tiles - 1:
                    qc = _chunk_of(lax.broadcasted_iota(jnp.int32, s.shape, 0))
                    kc = _chunk_of(lax.broadcasted_iota(jnp.int32, s.shape, 1))
                    s = jnp.where(kc <= qc, s, NEG_INF)
                s_ref[r_lo:r_hi, t * tk:(t + 1) * tk] = s
                for c in range(tk // LANES):
                    part = s[:, c * LANES:(c + 1) * LANES]
                    lane_max = part if lane_max is None else jnp.maximum(lane_max, part)
            m_old = m_ref[r_lo:r_hi, :]
            m_new = jnp.maximum(m_old, jnp.max(lane_max, axis=-1, keepdims=True))
            m_ref[r_lo:r_hi, :] = m_new
            return m_old, m_new

        def exponentials(g, m_new):
            s_ref, _, _ = per_head[g]
            p_blocks = []
            for r0 in blocks:
                m_blk = m_new[r0 - r_lo:r0 - r_lo + rb]
                p = jnp.exp2(s_ref[r0:r0 + rb, 0:width] - jnp.concatenate([m_blk] * (width // LANES), axis=1))
                p_blocks.append(p.astype(BF16))
            return jnp.concatenate(p_blocks, axis=0)

        def value_matmul(g, p, m_old, m_new):
            _, _, acc_ref = per_head[g]
            alpha = jnp.exp2(m_old - m_new)
            v_ones = jnp.concatenate([v_ref[g, pl.ds(k0, width), :], ones], axis=1)
            acc_ref[r_lo:r_hi, :] = (jnp.concatenate([alpha] * 2, axis=1) * acc_ref[r_lo:r_hi, :]
                                     + _dot(p, v_ones))

        maxima = [score_matmul(g) for g in range(heads)]
        probs = [exponentials(g, maxima[g][1]) for g in range(heads)]
        for g in range(heads):
            value_matmul(g, probs[g], *maxima[g])

    @pl.loop(0, qi)
    def _(jj):
        update(0, tq, jj * group, group, False)

    for d in range(group):
        update(d * tk, (d + 1) * tk, qi * group, d + 1, True)
    for g, (_, _, acc_ref) in enumerate(per_head):
        o_ref[:, g * V_HEAD_DIM:(g + 1) * V_HEAD_DIM] = (
            acc_ref[:, 0:V_HEAD_DIM] / acc_ref[:, V_HEAD_DIM:]).astype(o_ref.dtype)


def _attn_call(q, kt, v, *, batch, seq, tq, tk, heads):
    n = batch * seq
    nq = seq // tq
    assert kt.shape == (MLA_HEADS, n // tk, QK_DIM, tk) and tq % tk == 0
    return pl.pallas_call(
        functools.partial(_attn_kernel, tq=tq, tk=tk, heads=heads),
        grid=(batch, MLA_HEADS // heads, nq),
        in_specs=[
            pl.BlockSpec((heads, tq, QK_DIM), lambda b, h, i: (h, b * nq + i, 0)),
            pl.BlockSpec((heads, seq // tk, QK_DIM, tk), lambda b, h, i: (h, b, 0, 0)),
            pl.BlockSpec((heads, seq, V_HEAD_DIM), lambda b, h, i: (h, b, 0)),
        ],
        out_specs=pl.BlockSpec((tq, heads * V_HEAD_DIM), lambda b, h, i: (b * nq + i, h)),
        out_shape=jax.ShapeDtypeStruct((n, MLA_WIDTH), BF16),
        scratch_shapes=[
            pltpu.VMEM((tq, tq), F32),
            pltpu.VMEM((tq, LANES), F32),
            pltpu.VMEM((tq, V_HEAD_DIM + LANES), F32),
        ] * heads,
        compiler_params=pltpu.CompilerParams(
            dimension_semantics=("parallel", "parallel", "arbitrary"),
            vmem_limit_bytes=V7X_VMEM_LIMIT_BYTES),
        name="attn_prompt",
    )(q, kt, v)


def _attn_cache_kernel(q_ref, cckv_ref, ckpe_ref, nckv_ref, nkpe_ref, wuk_ref, wuv_ref, o_ref,
                       *, sq, past):
    ckv_c = cckv_ref[0].astype(BF16)
    kpe_c = ckpe_ref[0].astype(BF16)
    ckv_n = nckv_ref[...].astype(BF16)
    kpe_n = nkpe_ref[...].astype(BF16)
    rows = MLA_HEADS * sq
    q_pos = jnp.concatenate([past + lax.broadcasted_iota(jnp.int32, (sq, 1), 0)] * MLA_HEADS, axis=0)
    vis_c = _chunk_of(lax.broadcasted_iota(jnp.int32, (rows, past), 1)) <= _chunk_of(q_pos)
    vis_n = _chunk_of(past + lax.broadcasted_iota(jnp.int32, (rows, sq), 1)) <= _chunk_of(q_pos)
    q_lat = jnp.concatenate(
        [_dot_nt(q_ref[h][:, 0:QK_NOPE_DIM], wuk_ref[:, h * QK_NOPE_DIM:(h + 1) * QK_NOPE_DIM])
         for h in range(MLA_HEADS)], axis=0).astype(BF16)
    q_pe = jnp.concatenate([q_ref[h][:, QK_NOPE_DIM:QK_DIM] for h in range(MLA_HEADS)], axis=0)
    s_c = jnp.where(vis_c, _dot_nt(q_lat, ckv_c) + _dot_nt(q_pe, kpe_c), NEG_INF)
    s_n = jnp.where(vis_n, _dot_nt(q_lat, ckv_n) + _dot_nt(q_pe, kpe_n), NEG_INF)
    m = jnp.maximum(jnp.max(s_c, axis=-1, keepdims=True), jnp.max(s_n, axis=-1, keepdims=True))
    p_c = jnp.exp2(s_c - m)
    p_n = jnp.exp2(s_n - m)
    l = jnp.sum(p_c, axis=-1, keepdims=True) + jnp.sum(p_n, axis=-1, keepdims=True)
    o_lat = ((_dot(p_c.astype(BF16), ckv_c) + _dot(p_n.astype(BF16), ckv_n)) / l).astype(BF16)
    for h in range(MLA_HEADS):
        wuv_h = wuv_ref[:, h * V_HEAD_DIM:(h + 1) * V_HEAD_DIM]
        o_ref[:, h * V_HEAD_DIM:(h + 1) * V_HEAD_DIM] = _dot(o_lat[h * sq:(h + 1) * sq], wuv_h).astype(o_ref.dtype)


def _attn_cache_call(q, cache_ckv, cache_kpe, new_ckv, new_kpe, wuk, wuv, *, batch, sq):
    past = cache_ckv.shape[1]
    return pl.pallas_call(
        functools.partial(_attn_cache_kernel, sq=sq, past=past),
        grid=(batch,),
        in_specs=[
            pl.BlockSpec((MLA_HEADS, sq, QK_DIM), lambda b: (0, b, 0)),
            pl.BlockSpec((1, past, KV_LORA_RANK), lambda b: (b, 0, 0)),
            pl.BlockSpec((1, past, QK_ROPE_DIM), lambda b: (b, 0, 0)),
            pl.BlockSpec((sq, KV_LORA_RANK), lambda b: (b, 0)),
            pl.BlockSpec((sq, QK_ROPE_DIM), lambda b: (b, 0)),
            pl.BlockSpec(wuk.shape, lambda b: (0, 0)),
            pl.BlockSpec(wuv.shape, lambda b: (0, 0)),
        ],
        out_specs=pl.BlockSpec((sq, MLA_WIDTH), lambda b: (b, 0)),
        out_shape=jax.ShapeDtypeStruct((batch * sq, MLA_WIDTH), BF16),
        compiler_params=pltpu.CompilerParams(
            dimension_semantics=("parallel",), vmem_limit_bytes=V7X_VMEM_LIMIT_BYTES),
        name="attn_cache",
    )(q, cache_ckv, cache_kpe, new_ckv, new_kpe, wuk, wuv)


def _ffn_kernel(x_ref, gm_ref, mla_ref, cpast_ref, wo_ref, ln1g_ref, ln1b_ref, wup_ref, wconv_ref,
                bconv_ref, wdown_ref, ln2g_ref, ln2b_ref, y_ref, cstate_ref, carry_ref, act_ref,
                *, tm, seq_rows, tiles_per_seg, d_ff, cw, alpha):
    i = pl.program_id(0)
    n_seq = tm // seq_rows

    if n_seq == 1:
        @pl.when(i % tiles_per_seg == 0)
        def _():
            carry_ref[...] = cpast_ref[0]

    mix = _dot(gm_ref[...], wo_ref[0:GMLP_WIDTH, :]) + _dot(mla_ref[...], wo_ref[GMLP_WIDTH:, :])
    h = _layer_norm(alpha * x_ref[...] + mix, ln1g_ref[...], ln1b_ref[...])
    hb = h.astype(BF16)

    row = lax.broadcasted_iota(jnp.int32, (tm, cw), 0)

    def causal_conv(up, off):
        cols = slice(off, off + cw)
        s1, s2 = pltpu.roll(up, 1, 0), pltpu.roll(up, 2, 0)
        for k in range(n_seq):
            first = k * seq_rows
            history = carry_ref if n_seq == 1 else cpast_ref.at[k]
            prev2 = history[0:1, cols]
            prev1 = history[1:2, cols]
            s1 = jnp.where(row == first, prev1, s1)
            s2 = jnp.where(row == first, prev2, jnp.where(row == first + 1, prev1, s2))
            tail = up[first + seq_rows - (CONV_W - 1):first + seq_rows, :]
            if n_seq == 1:
                carry_ref[:, cols] = tail
            cstate_ref[k, :, cols] = tail
        conv = bconv_ref[:, cols] + wconv_ref[0:1, cols] * s2
        conv = conv + wconv_ref[1:2, cols] * s1
        return conv + wconv_ref[2:3, cols] * up

    for c in range(d_ff // cw):
        a = causal_conv(_dot(hb, wup_ref[:, c * cw:(c + 1) * cw]), c * cw)
        g = causal_conv(_dot(hb, wup_ref[:, d_ff + c * cw:d_ff + (c + 1) * cw]), d_ff + c * cw)
        act_ref[:, c * cw:(c + 1) * cw] = (a * (1.0 / (1.0 + jnp.exp(-a))) * g).astype(BF16)

    ff = _dot(act_ref[...], wdown_ref[...])
    y_ref[...] = _layer_norm(alpha * h + ff, ln2g_ref[...], ln2b_ref[...])


def _ffn_call(x, gm, mla, conv_past, wo, ln1g, ln1b, wup, wconv, bconv, wdown, ln2g, ln2b,
              *, tm, seq, alpha):
    n, d = x.shape
    d_ff = wdown.shape[0]
    batch = n // seq
    seq_rows = min(seq, tm)
    n_seq = tm // seq_rows
    tiles_per_seg = seq // seq_rows
    cw = FFN_COLUMN_CHUNK
    const = lambda *shape: pl.BlockSpec(shape, lambda i: (0,) * len(shape),
                                        pipeline_mode=pl.Buffered(1))
    return pl.pallas_call(
        functools.partial(_ffn_kernel, tm=tm, seq_rows=seq_rows, tiles_per_seg=tiles_per_seg, d_ff=d_ff,
                          cw=cw, alpha=alpha),
        grid=(n // tm,),
        in_specs=[
            pl.BlockSpec((tm, d), lambda i: (i, 0)),
            pl.BlockSpec((tm, GMLP_WIDTH), lambda i: (i, 0)),
            pl.BlockSpec((tm, MLA_WIDTH), lambda i: (i, 0)),
            pl.BlockSpec((n_seq, CONV_W - 1, 2 * d_ff), lambda i: (i // tiles_per_seg, 0, 0)),
            const(GMLP_WIDTH + MLA_WIDTH, d),
            const(1, d), const(1, d),
            const(d, 2 * d_ff),
            const(CONV_W, 2 * d_ff), const(1, 2 * d_ff),
            const(d_ff, d),
            const(1, d), const(1, d),
        ],
        out_specs=[
            pl.BlockSpec((tm, d), lambda i: (i, 0)),
            pl.BlockSpec((n_seq, CONV_W - 1, 2 * d_ff), lambda i: (i // tiles_per_seg, 0, 0)),
        ],
        out_shape=[
            jax.ShapeDtypeStruct((n, d), F32),
            jax.ShapeDtypeStruct((batch, CONV_W - 1, 2 * d_ff), F32),
        ],
        scratch_shapes=[
            pltpu.VMEM((CONV_W - 1, 2 * d_ff), F32),
            pltpu.VMEM((tm, d_ff), BF16),
        ],
        compiler_params=pltpu.CompilerParams(
            dimension_semantics=("arbitrary",), vmem_limit_bytes=V7X_VMEM_LIMIT_BYTES),
        name=f"ffn_tm{tm}",
    )(x, gm, mla, conv_past, wo, ln1g, ln1b, wup, wconv, bconv, wdown, ln2g, ln2b)


def _rope_tables(first_pos, n_pos, reps):
    inv_freq = jnp.power(ROPE_BASE, -jnp.arange(0, QK_ROPE_DIM, 2, dtype=F32) / QK_ROPE_DIM)
    ang = (first_pos + jnp.arange(n_pos)).astype(F32)[:, None] * inv_freq[None, :]
    cos, sin = jnp.cos(ang), jnp.sin(ang)
    return jnp.tile(jnp.concatenate([cos, cos, -sin, sin], axis=-1), (reps, 1))


def _swap_halves(w):
    half = w.shape[-1] // 2
    return jnp.concatenate([w[..., half:], w[..., :half]], axis=-1)


def _layer_weights(l, w_in, ln_v_g, ln_v_b, g_q, w_uq, g_kv, w_uk, w_uv, w_o, ln1_g, ln1_b, w_up,
                   w_conv, b_conv, w_down, ln2_g, ln2_b):
    row = lambda a: a[l].reshape(1, -1)
    o_kpe = 2 * GMLP_WIDTH + Q_LORA_RANK + KV_LORA_RANK
    w_in_ext = jnp.concatenate([w_in[l], _swap_halves(w_in[l][:, o_kpe:])], axis=1).astype(BF16)
    wq = w_uq[l]
    wuq_ext = jnp.concatenate([wq, _swap_halves(wq[..., QK_NOPE_DIM:])], axis=-1)
    wuq_ext = wuq_ext.reshape(Q_LORA_RANK, MLA_HEADS * Q_EXT_DIM).astype(BF16)
    return dict(
        w_in_ext=w_in_ext, lnv_g=row(ln_v_g), lnv_b=row(ln_v_b), g_q=row(g_q), wuq_ext=wuq_ext,
        g_kv=row(g_kv),
        wuk=w_uk[l].reshape(KV_LORA_RANK, MLA_HEADS * QK_NOPE_DIM).astype(BF16),
        wukt=w_uk[l].reshape(KV_LORA_RANK, MLA_HEADS * QK_NOPE_DIM).T.astype(BF16),
        wuv=w_uv[l].reshape(KV_LORA_RANK, MLA_WIDTH).astype(BF16),
        wo=w_o[l].astype(BF16), ln1g=row(ln1_g), ln1b=row(ln1_b), wup=w_up[l].astype(BF16),
        wconv=w_conv[l], bconv=row(b_conv), wdown=w_down[l].astype(BF16), ln2g=row(ln2_g),
        ln2b=row(ln2_b))


def _mix_operands(w_s, b_s, seg):
    reps = GMLP_CHUNK // seg
    wmix = jnp.tile(w_s[:, :seg, :seg], (1, reps, reps))
    bcol = jnp.tile(b_s[:, :seg].T, (reps, 1))
    return wmix, jnp.repeat(bcol, GMLP_HEAD_DIM, axis=1)


def _tiles(batch, seq):
    if seq >= PROMPT_TILE:
        return dict(proj=PROMPT_TILE, attn=ATTN_QUERY_TILE, ffn=PROMPT_TILE)
    assert (batch * seq) % GMLP_CHUNK == 0 and GMLP_CHUNK % seq == 0
    return dict(proj=batch * seq, attn=None, ffn=batch * seq)


def _trunk(x, w, w_s, b_s, *, batch, seq, past, conv_past, alpha, cache=None):
    tiles = _tiles(batch, seq)
    seg = min(seq, GMLP_CHUNK)
    wmix, bmix = _mix_operands(w_s, b_s, seg)
    rope_t = _rope_tables(past, seq, max(1, tiles["proj"] // seq))
    gm, q, kt, v, ckv, kpe, *vln = _proj_call(
        x, w["w_in_ext"], w["lnv_g"], w["lnv_b"], wmix, bmix, w["g_q"], w["wuq_ext"], w["g_kv"],
        w["wukt"], w["wuv"], rope_t, tm=tiles["proj"], seg=seg, emit_vln=cache is not None)
    if cache is None:
        mla = _attn_call(q, kt, v, batch=batch, seq=seq, tq=tiles["attn"], tk=tiles["proj"],
                         heads=ATTN_HEADS_PER_STEP)
    else:
        mla = _attn_cache_call(q, cache[0], cache[1], ckv, kpe, w["wuk"], w["wuv"], batch=batch, sq=seq)
    y, conv_state = _ffn_call(x, gm, mla, conv_past, w["wo"], w["ln1g"], w["ln1b"], w["wup"],
                              w["wconv"], w["bconv"], w["wdown"], w["ln2g"], w["ln2b"],
                              tm=tiles["ffn"], seq=seq, alpha=alpha)
    return y, ckv, kpe, (vln[0] if vln else None), conv_state


def kernel(x_prompt, x_sample, cache_ckv, cache_kpe, state_ffn_conv, w_in, ln_v_g, ln_v_b, w_s, b_s, g_q, w_uq, g_kv, w_uk, w_uv, w_o, ln1_g, ln1_b, w_up, w_conv, b_conv, w_down, ln2_g, ln2_b):
    depth = w_in.shape[0]
    alpha = (2.0 * depth) ** 0.25
    bp, sp, d = x_prompt.shape
    bs, ss, _ = x_sample.shape
    d_ff = w_down.shape[1]
    hp = x_prompt.reshape(bp * sp, d)
    hs = x_sample.reshape(bs * ss, d)
    outs = [[] for _ in range(7)]
    for l in range(depth):
        w = _layer_weights(l, w_in, ln_v_g, ln_v_b, g_q, w_uq, g_kv, w_uk, w_uv, w_o, ln1_g, ln1_b,
                           w_up, w_conv, b_conv, w_down, ln2_g, ln2_b)
        hp, ckv_p, kpe_p, _, conv_p = _trunk(
            hp, w, w_s[l], b_s[l], batch=bp, seq=sp, past=0,
            conv_past=jnp.zeros((bp, CONV_W - 1, 2 * d_ff), F32), alpha=alpha)
        hs, ckv_s, kpe_s, v_s, conv_s = _trunk(
            hs, w, w_s[l], b_s[l], batch=bs, seq=ss, past=cache_ckv.shape[2],
            conv_past=state_ffn_conv[l], alpha=alpha, cache=(cache_ckv[l], cache_kpe[l]))
        for acc, val in zip(outs, (
                ckv_p.reshape(bp, sp, -1), kpe_p.reshape(bp, sp, -1), conv_p,
                ckv_s.reshape(bs, ss, -1), kpe_s.reshape(bs, ss, -1), v_s.reshape(bs, ss, -1), conv_s)):
            acc.append(val)
    return (hp.reshape(bp, sp, d), hs.reshape(bs, ss, d), *(jnp.stack(o) for o in outs))
```

```python
import functools
import math

import jax
import jax.numpy as jnp
from jax import lax
from jax.experimental import pallas as pl
from jax.experimental.pallas import tpu as pltpu

CHUNK = 64
GMLP_CHUNK = 128
GMLP_HEADS = 4
GMLP_HEAD_DIM = 128
GMLP_WIDTH = GMLP_HEADS * GMLP_HEAD_DIM
MLA_HEADS = 4
QK_NOPE_DIM = 128
QK_ROPE_DIM = 64
QK_DIM = QK_NOPE_DIM + QK_ROPE_DIM
V_HEAD_DIM = 128
MLA_WIDTH = MLA_HEADS * V_HEAD_DIM
Q_LORA_RANK = 384
KV_LORA_RANK = 256
CONV_W = 3
ROPE_BASE = 10000.0
ATTN_SCALE = 1.0 / math.sqrt(QK_NOPE_DIM + QK_ROPE_DIM)
Q_SCALE = ATTN_SCALE * math.log2(math.e)
LN_EPS = 1e-5
RMS_EPS = 1e-6
NEG_INF = -1e30

Q_EXT_DIM = QK_NOPE_DIM + 2 * QK_ROPE_DIM

V7X_VMEM_LIMIT_BYTES = 56 * 1024 * 1024
LANES = 128
FFN_COLUMN_CHUNK = 256
assert V_HEAD_DIM == LANES
PROMPT_TILE = 512
ATTN_QUERY_TILE = 1024
ATTN_HEADS_PER_STEP = 2
ATTN_ROW_BLOCK = 32

F32 = jnp.float32
BF16 = jnp.bfloat16


def _layer_norm(x, g, b):
    mu = jnp.mean(x, axis=-1, keepdims=True)
    xc = x - mu
    var = jnp.mean(xc * xc, axis=-1, keepdims=True)
    return xc * lax.rsqrt(var + LN_EPS) * g + b


def _rms_norm(x, g):
    return x * lax.rsqrt(jnp.mean(x * x, axis=-1, keepdims=True) + RMS_EPS) * g


def _chunk_of(pos):
    return jnp.right_shift(pos, CHUNK.bit_length() - 1)


def _rope(pair, table):
    t = pair * table
    return t + pltpu.roll(t, QK_ROPE_DIM, 1)


def _dot(a, b):
    return jnp.dot(a, b, preferred_element_type=F32)


def _dot_nt(a, b):
    return lax.dot_general(a, b, (((1,), (1,)), ((), ())), preferred_element_type=F32)


def _proj_kernel(x_ref, w_in_ref, lnv_g_ref, lnv_b_ref, wmix_ref, bmix_ref, gq_ref, wuq_ref,
                 gkv_ref, wukt_ref, wuv_ref, rope_ref,
                 gm_ref, q_ref, kt_ref, v_ref, ckv_ref, kpe_ref, *maybe_vln_ref, tm, seg):
    x = x_ref[...].astype(BF16)
    proj = _dot(x, w_in_ref[...])
    o_v, o_cq = GMLP_WIDTH, 2 * GMLP_WIDTH
    o_ckv = o_cq + Q_LORA_RANK
    o_kpe = o_ckv + KV_LORA_RANK

    u = jax.nn.gelu(proj[:, 0:o_v])
    v = _layer_norm(jax.nn.gelu(proj[:, o_v:o_cq]), lnv_g_ref[...], lnv_b_ref[...])
    for vln_ref in maybe_vln_ref:
        vln_ref[...] = v
    vb = v.astype(BF16)
    row = lax.broadcasted_iota(jnp.int32, (GMLP_CHUNK, GMLP_CHUNK), 0)
    col = lax.broadcasted_iota(jnp.int32, (GMLP_CHUNK, GMLP_CHUNK), 1)
    same_chunk = jnp.bitwise_xor(row, col) < seg
    keep = same_chunk & (_chunk_of(jnp.bitwise_and(col, seg - 1)) <= _chunk_of(jnp.bitwise_and(row, seg - 1)))
    for h in range(GMLP_HEADS):
        cs = slice(h * GMLP_HEAD_DIM, (h + 1) * GMLP_HEAD_DIM)
        w_h = jnp.where(keep, wmix_ref[h], 0.0).astype(BF16)
        chunks = [slice(c * GMLP_CHUNK, (c + 1) * GMLP_CHUNK) for c in range(tm // GMLP_CHUNK)]
        mixed = _dot(w_h, jnp.concatenate([vb[rs, cs] for rs in chunks], axis=1))
        for c, rs in enumerate(chunks):
            s = mixed[:, c * GMLP_HEAD_DIM:(c + 1) * GMLP_HEAD_DIM] + bmix_ref[:, cs]
            gm_ref[rs, cs] = (u[rs, cs] * s).astype(BF16)

    rope = rope_ref[...]

    cqn = _rms_norm(proj[:, o_cq:o_ckv], gq_ref[...]).astype(BF16)
    q = _dot(cqn, wuq_ref[...])
    for h in range(MLA_HEADS):
        b0 = h * Q_EXT_DIM
        q_pe = _rope(q[:, b0 + QK_NOPE_DIM:b0 + Q_EXT_DIM], rope)[:, 0:QK_ROPE_DIM]
        q_ref[h, :, 0:QK_NOPE_DIM] = (q[:, b0:b0 + QK_NOPE_DIM] * Q_SCALE).astype(BF16)
        q_ref[h, :, QK_NOPE_DIM:QK_DIM] = (q_pe * Q_SCALE).astype(BF16)

    ckv = _rms_norm(proj[:, o_ckv:o_kpe], gkv_ref[...])
    ckv_ref[...] = ckv
    kpe2 = _rope(proj[:, o_kpe:o_kpe + 2 * QK_ROPE_DIM], rope)
    kpe_ref[...] = kpe2[:, 0:QK_ROPE_DIM]
    cb = ckv.astype(BF16)
    k_nope_t = _dot_nt(wukt_ref[...], cb)
    kpe_t = kpe2.T[0:QK_ROPE_DIM, :].astype(BF16)
    v_up = _dot(cb, wuv_ref[...])
    for h in range(MLA_HEADS):
        kt_ref[h, 0, 0:QK_NOPE_DIM, :] = k_nope_t[h * QK_NOPE_DIM:(h + 1) * QK_NOPE_DIM, :].astype(BF16)
        kt_ref[h, 0, QK_NOPE_DIM:QK_DIM, :] = kpe_t
        v_ref[h] = v_up[:, h * V_HEAD_DIM:(h + 1) * V_HEAD_DIM].astype(BF16)


def _proj_call(x, w_in_ext, lnv_g, lnv_b, wmix, bmix, g_q, wuq_ext, g_kv, wukt, wuv, rope_t,
               *, tm, seg, emit_vln):
    n, d = x.shape
    in_w = w_in_ext.shape[1]
    n_tiles = n // tm
    tbl_tiles = rope_t.shape[0] // tm
    const = lambda *shape: pl.BlockSpec(shape, lambda i: (0,) * len(shape))
    in_specs = [
        pl.BlockSpec((tm, d), lambda i: (i, 0)),
        const(d, in_w),
        const(1, GMLP_WIDTH), const(1, GMLP_WIDTH),
        const(GMLP_HEADS, GMLP_CHUNK, GMLP_CHUNK), const(GMLP_CHUNK, GMLP_WIDTH),
        const(1, Q_LORA_RANK), const(Q_LORA_RANK, MLA_HEADS * Q_EXT_DIM),
        const(1, KV_LORA_RANK), const(MLA_HEADS * QK_NOPE_DIM, KV_LORA_RANK),
        const(KV_LORA_RANK, MLA_WIDTH),
        pl.BlockSpec((tm, 2 * QK_ROPE_DIM), lambda i: (i % tbl_tiles, 0)),
    ]
    outs = [
        (jax.ShapeDtypeStruct((n, GMLP_WIDTH), BF16), pl.BlockSpec((tm, GMLP_WIDTH), lambda i: (i, 0))),
        (jax.ShapeDtypeStruct((MLA_HEADS, n, QK_DIM), BF16),
         pl.BlockSpec((MLA_HEADS, tm, QK_DIM), lambda i: (0, i, 0))),
        (jax.ShapeDtypeStruct((MLA_HEADS, n_tiles, QK_DIM, tm), BF16),
         pl.BlockSpec((MLA_HEADS, 1, QK_DIM, tm), lambda i: (0, i, 0, 0))),
        (jax.ShapeDtypeStruct((MLA_HEADS, n, V_HEAD_DIM), BF16),
         pl.BlockSpec((MLA_HEADS, tm, V_HEAD_DIM), lambda i: (0, i, 0))),
        (jax.ShapeDtypeStruct((n, KV_LORA_RANK), F32), pl.BlockSpec((tm, KV_LORA_RANK), lambda i: (i, 0))),
        (jax.ShapeDtypeStruct((n, QK_ROPE_DIM), F32), pl.BlockSpec((tm, QK_ROPE_DIM), lambda i: (i, 0))),
    ]
    if emit_vln:
        outs.append((jax.ShapeDtypeStruct((n, GMLP_WIDTH), F32),
                     pl.BlockSpec((tm, GMLP_WIDTH), lambda i: (i, 0))))
    return pl.pallas_call(
        functools.partial(_proj_kernel, tm=tm, seg=seg),
        grid=(n_tiles,),
        in_specs=in_specs,
        out_specs=[spec for _, spec in outs],
        out_shape=[shape for shape, _ in outs],
        compiler_params=pltpu.CompilerParams(
            dimension_semantics=("parallel",), vmem_limit_bytes=V7X_VMEM_LIMIT_BYTES),
        name=f"proj_tm{tm}",
    )(x, w_in_ext, lnv_g, lnv_b, wmix, bmix, g_q, wuq_ext, g_kv, wukt, wuv, rope_t)


def _attn_kernel(q_ref, kt_ref, v_ref, o_ref, *scratch, tq, tk, heads):
    qi = pl.program_id(2)
    rb = ATTN_ROW_BLOCK
    group = tq // tk
    per_head = [scratch[3 * g:3 * g + 3] for g in range(heads)]
    for _, m_ref, acc_ref in per_head:
        m_ref[...] = jnp.full(m_ref.shape, NEG_INF, F32)
        acc_ref[...] = jnp.zeros(acc_ref.shape, F32)

    def update(r_lo, r_hi, j0, n_tiles, last_is_diagonal):
        width = n_tiles * tk
        k0 = pl.multiple_of(j0 * tk, tk)
        ones = jnp.ones((width, LANES), BF16)
        blocks = range(r_lo, r_hi, rb)

        def score_matmul(g):
            s_ref, m_ref, _ = per_head[g]
            lane_max = None
            for t in range(n_tiles):
                s = _dot(q_ref[g, r_lo:r_hi, :], kt_ref[g, j0 + t])
                if last_is_diagonal and t == n_tiles - 1:
                    qc = _chunk_of(lax.broadcasted_iota(jnp.int32, s.shape, 0))
                    kc = _chunk_of(lax.broadcasted_iota(jnp.int32, s.shape, 1))
                    s = jnp.where(kc <= qc, s, NEG_INF)
                s_ref[r_lo:r_hi, t * tk:(t + 1) * tk] = s
                for c in range(tk // LANES):
                    part = s[:, c * LANES:(c + 1) * LANES]
                    lane_max = part if lane_max is None else jnp.maximum(lane_max, part)
            m_old = m_ref[r_lo:r_hi, :]
            m_new = jnp.maximum(m_old, jnp.max(lane_max, axis=-1, keepdims=True))
            m_ref[r_lo:r_hi, :] = m_new
            return m_old, m_new

        def exponentials(g, m_new):
            s_ref, _, _ = per_head[g]
            p_blocks = []
            for r0 in blocks:
                m_blk = m_new[r0 - r_lo:r0 - r_lo + rb]
                p = jnp.exp2(s_ref[r0:r0 + rb, 0:width] - jnp.concatenate([m_blk] * (width // LANES), axis=1))
                p_blocks.append(p.astype(BF16))
            return jnp.concatenate(p_blocks, axis=0)

        def value_matmul(g, p, m_old, m_new):
            _, _, acc_ref = per_head[g]
            alpha = jnp.exp2(m_old - m_new)
            v_ones = jnp.concatenate([v_ref[g, pl.ds(k0, width), :], ones], axis=1)
            acc_ref[r_lo:r_hi, :] = (jnp.concatenate([alpha] * 2, axis=1) * acc_ref[r_lo:r_hi, :]
                                     + _dot(p, v_ones))

        maxima = [score_matmul(g) for g in range(heads)]
        probs = [exponentials(g, maxima[g][1]) for g in range(heads)]
        for g in range(heads):
            value_matmul(g, probs[g], *maxima[g])

    @pl.loop(0, qi // 2)
    def _(jj):
        update(0, tq, jj * 2 * group, 2 * group, False)

    @pl.when(qi % 2 == 1)
    def _():
        update(0, tq, (qi - 1) * group, group, False)

    for d in range(group):
        update(d * tk, (d + 1) * tk, qi * group, d + 1, True)
    for g, (_, _, acc_ref) in enumerate(per_head):
        o_ref[:, g * V_HEAD_DIM:(g + 1) * V_HEAD_DIM] = (
            acc_ref[:, 0:V_HEAD_DIM] / acc_ref[:, V_HEAD_DIM:]).astype(o_ref.dtype)


def _attn_call(q, kt, v, *, batch, seq, tq, tk, heads):
    n = batch * seq
    nq = seq // tq
    assert kt.shape == (MLA_HEADS, n // tk, QK_DIM, tk) and tq % tk == 0
    return pl.pallas_call(
        functools.partial(_attn_kernel, tq=tq, tk=tk, heads=heads),
        grid=(batch, MLA_HEADS // heads, nq),
        in_specs=[
            pl.BlockSpec((heads, tq, QK_DIM), lambda b, h, i: (h, b * nq + i, 0)),
            pl.BlockSpec((heads, seq // tk, QK_DIM, tk), lambda b, h, i: (h, b, 0, 0)),
            pl.BlockSpec((heads, seq, V_HEAD_DIM), lambda b, h, i: (h, b, 0)),
        ],
        out_specs=pl.BlockSpec((tq, heads * V_HEAD_DIM), lambda b, h, i: (b * nq + i, h)),
        out_shape=jax.ShapeDtypeStruct((n, MLA_WIDTH), BF16),
        scratch_shapes=[
            pltpu.VMEM((tq, 2 * tq), F32),
            pltpu.VMEM((tq, LANES), F32),
            pltpu.VMEM((tq, V_HEAD_DIM + LANES), F32),
        ] * heads,
        compiler_params=pltpu.CompilerParams(
            dimension_semantics=("parallel", "parallel", "arbitrary"),
            vmem_limit_bytes=V7X_VMEM_LIMIT_BYTES),
        name="attn_prompt",
    )(q, kt, v)


def _attn_cache_kernel(q_ref, cckv_ref, ckpe_ref, nckv_ref, nkpe_ref, wuk_ref, wuv_ref, o_ref,
                       *, sq, past):
    ckv_c = cckv_ref[0].astype(BF16)
    kpe_c = ckpe_ref[0].astype(BF16)
    ckv_n = nckv_ref[...].astype(BF16)
    kpe_n = nkpe_ref[...].astype(BF16)
    rows = MLA_HEADS * sq
    q_pos = jnp.concatenate([past + lax.broadcasted_iota(jnp.int32, (sq, 1), 0)] * MLA_HEADS, axis=0)
    vis_c = _chunk_of(lax.broadcasted_iota(jnp.int32, (rows, past), 1)) <= _chunk_of(q_pos)
    vis_n = _chunk_of(past + lax.broadcasted_iota(jnp.int32, (rows, sq), 1)) <= _chunk_of(q_pos)
    q_lat = jnp.concatenate(
        [_dot_nt(q_ref[h][:, 0:QK_NOPE_DIM], wuk_ref[:, h * QK_NOPE_DIM:(h + 1) * QK_NOPE_DIM])
         for h in range(MLA_HEADS)], axis=0).astype(BF16)
    q_pe = jnp.concatenate([q_ref[h][:, QK_NOPE_DIM:QK_DIM] for h in range(MLA_HEADS)], axis=0)
    s_c = jnp.where(vis_c, _dot_nt(q_lat, ckv_c) + _dot_nt(q_pe, kpe_c), NEG_INF)
    s_n = jnp.where(vis_n, _dot_nt(q_lat, ckv_n) + _dot_nt(q_pe, kpe_n), NEG_INF)
    m = jnp.maximum(jnp.max(s_c, axis=-1, keepdims=True), jnp.max(s_n, axis=-1, keepdims=True))
    p_c = jnp.exp2(s_c - m)
    p_n = jnp.exp2(s_n - m)
    l = jnp.sum(p_c, axis=-1, keepdims=True) + jnp.sum(p_n, axis=-1, keepdims=True)
    o_lat = ((_dot(p_c.astype(BF16), ckv_c) + _dot(p_n.astype(BF16), ckv_n)) / l).astype(BF16)
    for h in range(MLA_HEADS):
        wuv_h = wuv_ref[:, h * V_HEAD_DIM:(h + 1) * V_HEAD_DIM]
        o_ref[:, h * V_HEAD_DIM:(h + 1) * V_HEAD_DIM] = _dot(o_lat[h * sq:(h + 1) * sq], wuv_h).astype(o_ref.dtype)


def _attn_cache_call(q, cache_ckv, cache_kpe, new_ckv, new_kpe, wuk, wuv, *, batch, sq):
    past = cache_ckv.shape[1]
    return pl.pallas_call(
        functools.partial(_attn_cache_kernel, sq=sq, past=past),
        grid=(batch,),
        in_specs=[
            pl.BlockSpec((MLA_HEADS, sq, QK_DIM), lambda b: (0, b, 0)),
            pl.BlockSpec((1, past, KV_LORA_RANK), lambda b: (b, 0, 0)),
            pl.BlockSpec((1, past, QK_ROPE_DIM), lambda b: (b, 0, 0)),
            pl.BlockSpec((sq, KV_LORA_RANK), lambda b: (b, 0)),
            pl.BlockSpec((sq, QK_ROPE_DIM), lambda b: (b, 0)),
            pl.BlockSpec(wuk.shape, lambda b: (0, 0)),
            pl.BlockSpec(wuv.shape, lambda b: (0, 0)),
        ],
        out_specs=pl.BlockSpec((sq, MLA_WIDTH), lambda b: (b, 0)),
        out_shape=jax.ShapeDtypeStruct((batch * sq, MLA_WIDTH), BF16),
        compiler_params=pltpu.CompilerParams(
            dimension_semantics=("parallel",), vmem_limit_bytes=V7X_VMEM_LIMIT_BYTES),
        name="attn_cache",
    )(q, cache_ckv, cache_kpe, new_ckv, new_kpe, wuk, wuv)


def _ffn_kernel(x_ref, gm_ref, mla_ref, cpast_ref, wo_ref, ln1g_ref, ln1b_ref, wup_ref, wconv_ref,
                bconv_ref, wdown_ref, ln2g_ref, ln2b_ref, y_ref, cstate_ref, carry_ref, act_ref,
                *, tm, seq_rows, tiles_per_seg, d_ff, cw, alpha):
    i = pl.program_id(0)
    n_seq = tm // seq_rows

    if n_seq == 1:
        @pl.when(i % tiles_per_seg == 0)
        def _():
            carry_ref[...] = cpast_ref[0]

    mix = _dot(gm_ref[...], wo_ref[0:GMLP_WIDTH, :]) + _dot(mla_ref[...], wo_ref[GMLP_WIDTH:, :])
    h = _layer_norm(alpha * x_ref[...] + mix, ln1g_ref[...], ln1b_ref[...])
    hb = h.astype(BF16)

    row = lax.broadcasted_iota(jnp.int32, (tm, cw), 0)

    def causal_conv(up, off):
        cols = slice(off, off + cw)
        s1, s2 = pltpu.roll(up, 1, 0), pltpu.roll(up, 2, 0)
        for k in range(n_seq):
            first = k * seq_rows
            history = carry_ref if n_seq == 1 else cpast_ref.at[k]
            prev2 = history[0:1, cols]
            prev1 = history[1:2, cols]
            s1 = jnp.where(row == first, prev1, s1)
            s2 = jnp.where(row == first, prev2, jnp.where(row == first + 1, prev1, s2))
            tail = up[first + seq_rows - (CONV_W - 1):first + seq_rows, :]
            if n_seq == 1:
                carry_ref[:, cols] = tail
            cstate_ref[k, :, cols] = tail
        conv = bconv_ref[:, cols] + wconv_ref[0:1, cols] * s2
        conv = conv + wconv_ref[1:2, cols] * s1
        return conv + wconv_ref[2:3, cols] * up

    for c in range(d_ff // cw):
        a = causal_conv(_dot(hb, wup_ref[:, c * cw:(c + 1) * cw]), c * cw)
        g = causal_conv(_dot(hb, wup_ref[:, d_ff + c * cw:d_ff + (c + 1) * cw]), d_ff + c * cw)
        act_ref[:, c * cw:(c + 1) * cw] = (a * (1.0 / (1.0 + jnp.exp(-a))) * g).astype(BF16)

    ff = _dot(act_ref[...], wdown_ref[...])
    y_ref[...] = _layer_norm(alpha * h + ff, ln2g_ref[...], ln2b_ref[...])


def _ffn_call(x, gm, mla, conv_past, wo, ln1g, ln1b, wup, wconv, bconv, wdown, ln2g, ln2b,
              *, tm, seq, alpha):
    n, d = x.shape
    d_ff = wdown.shape[0]
    batch = n // seq
    seq_rows = min(seq, tm)
    n_seq = tm // seq_rows
    tiles_per_seg = seq // seq_rows
    cw = FFN_COLUMN_CHUNK
    const = lambda *shape: pl.BlockSpec(shape, lambda i: (0,) * len(shape),
                                        pipeline_mode=pl.Buffered(1))
    return pl.pallas_call(
        functools.partial(_ffn_kernel, tm=tm, seq_rows=seq_rows, tiles_per_seg=tiles_per_seg, d_ff=d_ff,
                          cw=cw, alpha=alpha),
        grid=(n // tm,),
        in_specs=[
            pl.BlockSpec((tm, d), lambda i: (i, 0)),
            pl.BlockSpec((tm, GMLP_WIDTH), lambda i: (i, 0)),
            pl.BlockSpec((tm, MLA_WIDTH), lambda i: (i, 0)),
            pl.BlockSpec((n_seq, CONV_W - 1, 2 * d_ff), lambda i: (i // tiles_per_seg, 0, 0)),
            const(GMLP_WIDTH + MLA_WIDTH, d),
            const(1, d), const(1, d),
            const(d, 2 * d_ff),
            const(CONV_W, 2 * d_ff), const(1, 2 * d_ff),
            const(d_ff, d),
            const(1, d), const(1, d),
        ],
        out_specs=[
            pl.BlockSpec((tm, d), lambda i: (i, 0)),
            pl.BlockSpec((n_seq, CONV_W - 1, 2 * d_ff), lambda i: (i // tiles_per_seg, 0, 0)),
        ],
        out_shape=[
            jax.ShapeDtypeStruct((n, d), F32),
            jax.ShapeDtypeStruct((batch, CONV_W - 1, 2 * d_ff), F32),
        ],
        scratch_shapes=[
            pltpu.VMEM((CONV_W - 1, 2 * d_ff), F32),
            pltpu.VMEM((tm, d_ff), BF16),
        ],
        compiler_params=pltpu.CompilerParams(
            dimension_semantics=("arbitrary",), vmem_limit_bytes=V7X_VMEM_LIMIT_BYTES),
        name=f"ffn_tm{tm}",
    )(x, gm, mla, conv_past, wo, ln1g, ln1b, wup, wconv, bconv, wdown, ln2g, ln2b)


def _rope_tables(first_pos, n_pos, reps):
    inv_freq = jnp.power(ROPE_BASE, -jnp.arange(0, QK_ROPE_DIM, 2, dtype=F32) / QK_ROPE_DIM)
    ang = (first_pos + jnp.arange(n_pos)).astype(F32)[:, None] * inv_freq[None, :]
    cos, sin = jnp.cos(ang), jnp.sin(ang)
    return jnp.tile(jnp.concatenate([cos, cos, -sin, sin], axis=-1), (reps, 1))


def _swap_halves(w):
    half = w.shape[-1] // 2
    return jnp.concatenate([w[..., half:], w[..., :half]], axis=-1)


def _layer_weights(l, w_in, ln_v_g, ln_v_b, g_q, w_uq, g_kv, w_uk, w_uv, w_o, ln1_g, ln1_b, w_up,
                   w_conv, b_conv, w_down, ln2_g, ln2_b):
    row = lambda a: a[l].reshape(1, -1)
    o_kpe = 2 * GMLP_WIDTH + Q_LORA_RANK + KV_LORA_RANK
    w_in_ext = jnp.concatenate([w_in[l], _swap_halves(w_in[l][:, o_kpe:])], axis=1).astype(BF16)
    wq = w_uq[l]
    wuq_ext = jnp.concatenate([wq, _swap_halves(wq[..., QK_NOPE_DIM:])], axis=-1)
    wuq_ext = wuq_ext.reshape(Q_LORA_RANK, MLA_HEADS * Q_EXT_DIM).astype(BF16)
    return dict(
        w_in_ext=w_in_ext, lnv_g=row(ln_v_g), lnv_b=row(ln_v_b), g_q=row(g_q), wuq_ext=wuq_ext,
        g_kv=row(g_kv),
        wuk=w_uk[l].reshape(KV_LORA_RANK, MLA_HEADS * QK_NOPE_DIM).astype(BF16),
        wukt=w_uk[l].reshape(KV_LORA_RANK, MLA_HEADS * QK_NOPE_DIM).T.astype(BF16),
        wuv=w_uv[l].reshape(KV_LORA_RANK, MLA_WIDTH).astype(BF16),
        wo=w_o[l].astype(BF16), ln1g=row(ln1_g), ln1b=row(ln1_b), wup=w_up[l].astype(BF16),
        wconv=w_conv[l], bconv=row(b_conv), wdown=w_down[l].astype(BF16), ln2g=row(ln2_g),
        ln2b=row(ln2_b))


def _mix_operands(w_s, b_s, seg):
    reps = GMLP_CHUNK // seg
    wmix = jnp.tile(w_s[:, :seg, :seg], (1, reps, reps))
    bcol = jnp.tile(b_s[:, :seg].T, (reps, 1))
    return wmix, jnp.repeat(bcol, GMLP_HEAD_DIM, axis=1)


def _tiles(batch, seq):
    if seq >= PROMPT_TILE:
        return dict(proj=PROMPT_TILE, attn=ATTN_QUERY_TILE, ffn=PROMPT_TILE)
    assert (batch * seq) % GMLP_CHUNK == 0 and GMLP_CHUNK % seq == 0
    return dict(proj=batch * seq, attn=None, ffn=batch * seq)


def _trunk(x, w, w_s, b_s, *, batch, seq, past, conv_past, alpha, cache=None):
    tiles = _tiles(batch, seq)
    seg = min(seq, GMLP_CHUNK)
    wmix, bmix = _mix_operands(w_s, b_s, seg)
    rope_t = _rope_tables(past, seq, max(1, tiles["proj"] // seq))
    gm, q, kt, v, ckv, kpe, *vln = _proj_call(
        x, w["w_in_ext"], w["lnv_g"], w["lnv_b"], wmix, bmix, w["g_q"], w["wuq_ext"], w["g_kv"],
        w["wukt"], w["wuv"], rope_t, tm=tiles["proj"], seg=seg, emit_vln=cache is not None)
    if cache is None:
        mla = _attn_call(q, kt, v, batch=batch, seq=seq, tq=tiles["attn"], tk=tiles["proj"],
                         heads=ATTN_HEADS_PER_STEP)
    else:
        mla = _attn_cache_call(q, cache[0], cache[1], ckv, kpe, w["wuk"], w["wuv"], batch=batch, sq=seq)
    y, conv_state = _ffn_call(x, gm, mla, conv_past, w["wo"], w["ln1g"], w["ln1b"], w["wup"],
                              w["wconv"], w["bconv"], w["wdown"], w["ln2g"], w["ln2b"],
                              tm=tiles["ffn"], seq=seq, alpha=alpha)
    return y, ckv, kpe, (vln[0] if vln else None), conv_state


def kernel(x_prompt, x_sample, cache_ckv, cache_kpe, state_ffn_conv, w_in, ln_v_g, ln_v_b, w_s, b_s, g_q, w_uq, g_kv, w_uk, w_uv, w_o, ln1_g, ln1_b, w_up, w_conv, b_conv, w_down, ln2_g, ln2_b):
    depth = w_in.shape[0]
    alpha = (2.0 * depth) ** 0.25
    bp, sp, d = x_prompt.shape
    bs, ss, _ = x_sample.shape
    d_ff = w_down.shape[1]
    hp = x_prompt.reshape(bp * sp, d)
    hs = x_sample.reshape(bs * ss, d)
    outs = [[] for _ in range(7)]
    for l in range(depth):
        w = _layer_weights(l, w_in, ln_v_g, ln_v_b, g_q, w_uq, g_kv, w_uk, w_uv, w_o, ln1_g, ln1_b,
                           w_up, w_conv, b_conv, w_down, ln2_g, ln2_b)
        hp, ckv_p, kpe_p, _, conv_p = _trunk(
            hp, w, w_s[l], b_s[l], batch=bp, seq=sp, past=0,
            conv_past=jnp.zeros((bp, CONV_W - 1, 2 * d_ff), F32), alpha=alpha)
        hs, ckv_s, kpe_s, v_s, conv_s = _trunk(
            hs, w, w_s[l], b_s[l], batch=bs, seq=ss, past=cache_ckv.shape[2],
            conv_past=state_ffn_conv[l], alpha=alpha, cache=(cache_ckv[l], cache_kpe[l]))
        for acc, val in zip(outs, (
                ckv_p.reshape(bp, sp, -1), kpe_p.reshape(bp, sp, -1), conv_p,
                ckv_s.reshape(bs, ss, -1), kpe_s.reshape(bs, ss, -1), v_s.reshape(bs, ss, -1), conv_s)):
            acc.append(val)
    return (hp.reshape(bp, sp, d), hs.reshape(bs, ss, d), *(jnp.stack(o) for o in outs))
```

```python
import functools
import math

import jax
import jax.numpy as jnp
from jax import lax
from jax.experimental import pallas as pl
from jax.experimental.pallas import tpu as pltpu

CHUNK = 64
GMLP_CHUNK = 128
GMLP_HEADS = 4
GMLP_HEAD_DIM = 128
GMLP_WIDTH = GMLP_HEADS * GMLP_HEAD_DIM
MLA_HEADS = 4
QK_NOPE_DIM = 128
QK_ROPE_DIM = 64
QK_DIM = QK_NOPE_DIM + QK_ROPE_DIM
V_HEAD_DIM = 128
MLA_WIDTH = MLA_HEADS * V_HEAD_DIM
Q_LORA_RANK = 384
KV_LORA_RANK = 256
CONV_W = 3
ROPE_BASE = 10000.0
ATTN_SCALE = 1.0 / math.sqrt(QK_NOPE_DIM + QK_ROPE_DIM)
Q_SCALE = ATTN_SCALE * math.log2(math.e)
LN_EPS = 1e-5
RMS_EPS = 1e-6
NEG_INF = -1e30

Q_EXT_DIM = QK_NOPE_DIM + 2 * QK_ROPE_DIM

V7X_VMEM_LIMIT_BYTES = 56 * 1024 * 1024
LANES = 128
FFN_COLUMN_CHUNK = 256
assert V_HEAD_DIM == LANES
PROMPT_TILE = 512
ATTN_QUERY_TILE = 1024
ATTN_HEADS_PER_STEP = 2
ATTN_ROW_BLOCK = 32

F32 = jnp.float32
BF16 = jnp.bfloat16


def _layer_norm(x, g, b):
    mu = jnp.mean(x, axis=-1, keepdims=True)
    xc = x - mu
    var = jnp.mean(xc * xc, axis=-1, keepdims=True)
    return xc * lax.rsqrt(var + LN_EPS) * g + b


def _rms_norm(x, g):
    return x * lax.rsqrt(jnp.mean(x * x, axis=-1, keepdims=True) + RMS_EPS) * g


def _chunk_of(pos):
    return jnp.right_shift(pos, CHUNK.bit_length() - 1)


def _rope(pair, table):
    t = pair * table
    return t + pltpu.roll(t, QK_ROPE_DIM, 1)


def _dot(a, b):
    return jnp.dot(a, b, preferred_element_type=F32)


def _dot_nt(a, b):
    return lax.dot_general(a, b, (((1,), (1,)), ((), ())), preferred_element_type=F32)


def _proj_kernel(x_ref, w_in_ref, lnv_g_ref, lnv_b_ref, wmix_ref, bmix_ref, gq_ref, wuq_ref,
                 gkv_ref, wukt_ref, wuv_ref, rope_ref,
                 gm_ref, q_ref, kt_ref, v_ref, ckv_ref, kpe_ref, *maybe_vln_ref, tm, seg):
    x = x_ref[...].astype(BF16)
    proj = _dot(x, w_in_ref[...])
    o_v, o_cq = GMLP_WIDTH, 2 * GMLP_WIDTH
    o_ckv = o_cq + Q_LORA_RANK
    o_kpe = o_ckv + KV_LORA_RANK

    u = jax.nn.gelu(proj[:, 0:o_v])
    v = _layer_norm(jax.nn.gelu(proj[:, o_v:o_cq]), lnv_g_ref[...], lnv_b_ref[...])
    for vln_ref in maybe_vln_ref:
        vln_ref[...] = v
    vb = v.astype(BF16)
    row = lax.broadcasted_iota(jnp.int32, (GMLP_CHUNK, GMLP_CHUNK), 0)
    col = lax.broadcasted_iota(jnp.int32, (GMLP_CHUNK, GMLP_CHUNK), 1)
    same_chunk = jnp.bitwise_xor(row, col) < seg
    keep = same_chunk & (_chunk_of(jnp.bitwise_and(col, seg - 1)) <= _chunk_of(jnp.bitwise_and(row, seg - 1)))
    for h in range(GMLP_HEADS):
        cs = slice(h * GMLP_HEAD_DIM, (h + 1) * GMLP_HEAD_DIM)
        w_h = jnp.where(keep, wmix_ref[h], 0.0).astype(BF16)
        chunks = [slice(c * GMLP_CHUNK, (c + 1) * GMLP_CHUNK) for c in range(tm // GMLP_CHUNK)]
        mixed = _dot(w_h, jnp.concatenate([vb[rs, cs] for rs in chunks], axis=1))
        for c, rs in enumerate(chunks):
            s = mixed[:, c * GMLP_HEAD_DIM:(c + 1) * GMLP_HEAD_DIM] + bmix_ref[:, cs]
            gm_ref[rs, cs] = (u[rs, cs] * s).astype(BF16)

    rope = rope_ref[...]

    cqn = _rms_norm(proj[:, o_cq:o_ckv], gq_ref[...]).astype(BF16)
    q = _dot(cqn, wuq_ref[...])
    for h in range(MLA_HEADS):
        b0 = h * Q_EXT_DIM
        q_pe = _rope(q[:, b0 + QK_NOPE_DIM:b0 + Q_EXT_DIM], rope)[:, 0:QK_ROPE_DIM]
        q_ref[h, :, 0:QK_NOPE_DIM] = (q[:, b0:b0 + QK_NOPE_DIM] * Q_SCALE).astype(BF16)
        q_ref[h, :, QK_NOPE_DIM:QK_DIM] = (q_pe * Q_SCALE).astype(BF16)

    ckv = _rms_norm(proj[:, o_ckv:o_kpe], gkv_ref[...])
    ckv_ref[...] = ckv
    kpe2 = _rope(proj[:, o_kpe:o_kpe + 2 * QK_ROPE_DIM], rope)
    kpe_ref[...] = kpe2[:, 0:QK_ROPE_DIM]
    cb = ckv.astype(BF16)
    k_nope_t = _dot_nt(wukt_ref[...], cb)
    kpe_t = kpe2.T[0:QK_ROPE_DIM, :].astype(BF16)
    v_up = _dot(cb, wuv_ref[...])
    for h in range(MLA_HEADS):
        kt_ref[h, 0, 0:QK_NOPE_DIM, :] = k_nope_t[h * QK_NOPE_DIM:(h + 1) * QK_NOPE_DIM, :].astype(BF16)
        kt_ref[h, 0, QK_NOPE_DIM:QK_DIM, :] = kpe_t
        v_ref[h] = v_up[:, h * V_HEAD_DIM:(h + 1) * V_HEAD_DIM].astype(BF16)


def _proj_call(x, w_in_ext, lnv_g, lnv_b, wmix, bmix, g_q, wuq_ext, g_kv, wukt, wuv, rope_t,
               *, tm, seg, emit_vln):
    n, d = x.shape
    in_w = w_in_ext.shape[1]
    n_tiles = n // tm
    tbl_tiles = rope_t.shape[0] // tm
    const = lambda *shape: pl.BlockSpec(shape, lambda i: (0,) * len(shape))
    in_specs = [
        pl.BlockSpec((tm, d), lambda i: (i, 0)),
        const(d, in_w),
        const(1, GMLP_WIDTH), const(1, GMLP_WIDTH),
        const(GMLP_HEADS, GMLP_CHUNK, GMLP_CHUNK), const(GMLP_CHUNK, GMLP_WIDTH),
        const(1, Q_LORA_RANK), const(Q_LORA_RANK, MLA_HEADS * Q_EXT_DIM),
        const(1, KV_LORA_RANK), const(MLA_HEADS * QK_NOPE_DIM, KV_LORA_RANK),
        const(KV_LORA_RANK, MLA_WIDTH),
        pl.BlockSpec((tm, 2 * QK_ROPE_DIM), lambda i: (i % tbl_tiles, 0)),
    ]
    outs = [
        (jax.ShapeDtypeStruct((n, GMLP_WIDTH), BF16), pl.BlockSpec((tm, GMLP_WIDTH), lambda i: (i, 0))),
        (jax.ShapeDtypeStruct((MLA_HEADS, n, QK_DIM), BF16),
         pl.BlockSpec((MLA_HEADS, tm, QK_DIM), lambda i: (0, i, 0))),
        (jax.ShapeDtypeStruct((MLA_HEADS, n_tiles, QK_DIM, tm), BF16),
         pl.BlockSpec((MLA_HEADS, 1, QK_DIM, tm), lambda i: (0, i, 0, 0))),
        (jax.ShapeDtypeStruct((MLA_HEADS, n, V_HEAD_DIM), BF16),
         pl.BlockSpec((MLA_HEADS, tm, V_HEAD_DIM), lambda i: (0, i, 0))),
        (jax.ShapeDtypeStruct((n, KV_LORA_RANK), F32), pl.BlockSpec((tm, KV_LORA_RANK), lambda i: (i, 0))),
        (jax.ShapeDtypeStruct((n, QK_ROPE_DIM), F32), pl.BlockSpec((tm, QK_ROPE_DIM), lambda i: (i, 0))),
    ]
    if emit_vln:
        outs.append((jax.ShapeDtypeStruct((n, GMLP_WIDTH), F32),
                     pl.BlockSpec((tm, GMLP_WIDTH), lambda i: (i, 0))))
    return pl.pallas_call(
        functools.partial(_proj_kernel, tm=tm, seg=seg),
        grid=(n_tiles,),
        in_specs=in_specs,
        out_specs=[spec for _, spec in outs],
        out_shape=[shape for shape, _ in outs],
        compiler_params=pltpu.CompilerParams(
            dimension_semantics=("parallel",), vmem_limit_bytes=V7X_VMEM_LIMIT_BYTES),
        name=f"proj_tm{tm}",
    )(x, w_in_ext, lnv_g, lnv_b, wmix, bmix, g_q, wuq_ext, g_kv, wukt, wuv, rope_t)


def _attn_kernel(q_ref, kt_ref, v_ref, o_ref, *scratch, tq, tk, heads):
    qi = pl.program_id(2)
    rb = ATTN_ROW_BLOCK
    group = tq // tk
    per_head = [scratch[3 * g:3 * g + 3] for g in range(heads)]
    for _, m_ref, acc_ref in per_head:
        m_ref[...] = jnp.full(m_ref.shape, NEG_INF, F32)
        acc_ref[...] = jnp.zeros(acc_ref.shape, F32)

    def update(r_lo, r_hi, j0, n_tiles, last_is_diagonal):
        width = n_tiles * tk
        k0 = pl.multiple_of(j0 * tk, tk)
        ones = jnp.ones((width, LANES), BF16)
        blocks = range(r_lo, r_hi, rb)

        def score_matmul(g):
            s_ref, m_ref, _ = per_head[g]
            lane_max = None
            for t in range(n_tiles):
                s = _dot(q_ref[g, r_lo:r_hi, :], kt_ref[g, j0 + t])
                if last_is_diagonal and t == n_tiles - 1:
                    qc = _chunk_of(lax.broadcasted_iota(jnp.int32, s.shape, 0))
                    kc = _chunk_of(lax.broadcasted_iota(jnp.int32, s.shape, 1))
                    s = jnp.where(kc <= qc, s, NEG_INF)
                s_ref[r_lo:r_hi, t * tk:(t + 1) * tk] = s
                for c in range(tk // LANES):
                    part = s[:, c * LANES:(c + 1) * LANES]
                    lane_max = part if lane_max is None else jnp.maximum(lane_max, part)
            m_old = m_ref[r_lo:r_hi, :]
            m_new = jnp.maximum(m_old, jnp.max(lane_max, axis=-1, keepdims=True))
            m_ref[r_lo:r_hi, :] = m_new
            return m_old, m_new

        def exponentials(g, m_new):
            s_ref, _, _ = per_head[g]
            p_blocks = []
            for r0 in blocks:
                m_blk = m_new[r0 - r_lo:r0 - r_lo + rb]
                p = jnp.exp2(s_ref[r0:r0 + rb, 0:width] - jnp.concatenate([m_blk] * (width // LANES), axis=1))
                p_blocks.append(p.astype(BF16))
            return jnp.concatenate(p_blocks, axis=0)

        def value_matmul(g, p, m_old, m_new):
            _, _, acc_ref = per_head[g]
            alpha = jnp.exp2(m_old - m_new)
            v_ones = jnp.concatenate([v_ref[g, pl.ds(k0, width), :], ones], axis=1)
            acc_ref[r_lo:r_hi, :] = (jnp.concatenate([alpha] * 2, axis=1) * acc_ref[r_lo:r_hi, :]
                                     + _dot(p, v_ones))

        maxima = [score_matmul(g) for g in range(heads)]
        probs = [exponentials(g, maxima[g][1]) for g in range(heads)]
        for g in range(heads):
            value_matmul(g, probs[g], *maxima[g])

    @pl.loop(0, qi // 2)
    def _(jj):
        update(0, tq, jj * 2 * group, 2 * group, False)

    @pl.when(qi % 2 == 1)
    def _():
        for d in range(group):
            update(d * tk, (d + 1) * tk, (qi - 1) * group, group + d + 1, True)

    @pl.when(qi % 2 == 0)
    def _():
        for d in range(group):
            update(d * tk, (d + 1) * tk, qi * group, d + 1, True)
    for g, (_, _, acc_ref) in enumerate(per_head):
        o_ref[:, g * V_HEAD_DIM:(g + 1) * V_HEAD_DIM] = (
            acc_ref[:, 0:V_HEAD_DIM] / acc_ref[:, V_HEAD_DIM:]).astype(o_ref.dtype)


def _attn_call(q, kt, v, *, batch, seq, tq, tk, heads):
    n = batch * seq
    nq = seq // tq
    assert kt.shape == (MLA_HEADS, n // tk, QK_DIM, tk) and tq % tk == 0
    return pl.pallas_call(
        functools.partial(_attn_kernel, tq=tq, tk=tk, heads=heads),
        grid=(batch, MLA_HEADS // heads, nq),
        in_specs=[
            pl.BlockSpec((heads, tq, QK_DIM), lambda b, h, i: (h, b * nq + i, 0)),
            pl.BlockSpec((heads, seq // tk, QK_DIM, tk), lambda b, h, i: (h, b, 0, 0)),
            pl.BlockSpec((heads, seq, V_HEAD_DIM), lambda b, h, i: (h, b, 0)),
        ],
        out_specs=pl.BlockSpec((tq, heads * V_HEAD_DIM), lambda b, h, i: (b * nq + i, h)),
        out_shape=jax.ShapeDtypeStruct((n, MLA_WIDTH), BF16),
        scratch_shapes=[
            pltpu.VMEM((tq, 2 * tq), F32),
            pltpu.VMEM((tq, LANES), F32),
            pltpu.VMEM((tq, V_HEAD_DIM + LANES), F32),
        ] * heads,
        compiler_params=pltpu.CompilerParams(
            dimension_semantics=("parallel", "parallel", "arbitrary"),
            vmem_limit_bytes=V7X_VMEM_LIMIT_BYTES),
        name="attn_prompt",
    )(q, kt, v)


def _attn_cache_kernel(q_ref, cckv_ref, ckpe_ref, nckv_ref, nkpe_ref, wuk_ref, wuv_ref, o_ref,
                       *, sq, past):
    ckv_c = cckv_ref[0].astype(BF16)
    kpe_c = ckpe_ref[0].astype(BF16)
    ckv_n = nckv_ref[...].astype(BF16)
    kpe_n = nkpe_ref[...].astype(BF16)
    rows = MLA_HEADS * sq
    q_pos = jnp.concatenate([past + lax.broadcasted_iota(jnp.int32, (sq, 1), 0)] * MLA_HEADS, axis=0)
    vis_c = _chunk_of(lax.broadcasted_iota(jnp.int32, (rows, past), 1)) <= _chunk_of(q_pos)
    vis_n = _chunk_of(past + lax.broadcasted_iota(jnp.int32, (rows, sq), 1)) <= _chunk_of(q_pos)
    q_lat = jnp.concatenate(
        [_dot_nt(q_ref[h][:, 0:QK_NOPE_DIM], wuk_ref[:, h * QK_NOPE_DIM:(h + 1) * QK_NOPE_DIM])
         for h in range(MLA_HEADS)], axis=0).astype(BF16)
    q_pe = jnp.concatenate([q_ref[h][:, QK_NOPE_DIM:QK_DIM] for h in range(MLA_HEADS)], axis=0)
    s_c = jnp.where(vis_c, _dot_nt(q_lat, ckv_c) + _dot_nt(q_pe, kpe_c), NEG_INF)
    s_n = jnp.where(vis_n, _dot_nt(q_lat, ckv_n) + _dot_nt(q_pe, kpe_n), NEG_INF)
    m = jnp.maximum(jnp.max(s_c, axis=-1, keepdims=True), jnp.max(s_n, axis=-1, keepdims=True))
    p_c = jnp.exp2(s_c - m)
    p_n = jnp.exp2(s_n - m)
    l = jnp.sum(p_c, axis=-1, keepdims=True) + jnp.sum(p_n, axis=-1, keepdims=True)
    o_lat = ((_dot(p_c.astype(BF16), ckv_c) + _dot(p_n.astype(BF16), ckv_n)) / l).astype(BF16)
    for h in range(MLA_HEADS):
        wuv_h = wuv_ref[:, h * V_HEAD_DIM:(h + 1) * V_HEAD_DIM]
        o_ref[:, h * V_HEAD_DIM:(h + 1) * V_HEAD_DIM] = _dot(o_lat[h * sq:(h + 1) * sq], wuv_h).astype(o_ref.dtype)


def _attn_cache_call(q, cache_ckv, cache_kpe, new_ckv, new_kpe, wuk, wuv, *, batch, sq):
    past = cache_ckv.shape[1]
    return pl.pallas_call(
        functools.partial(_attn_cache_kernel, sq=sq, past=past),
        grid=(batch,),
        in_specs=[
            pl.BlockSpec((MLA_HEADS, sq, QK_DIM), lambda b: (0, b, 0)),
            pl.BlockSpec((1, past, KV_LORA_RANK), lambda b: (b, 0, 0)),
            pl.BlockSpec((1, past, QK_ROPE_DIM), lambda b: (b, 0, 0)),
            pl.BlockSpec((sq, KV_LORA_RANK), lambda b: (b, 0)),
            pl.BlockSpec((sq, QK_ROPE_DIM), lambda b: (b, 0)),
            pl.BlockSpec(wuk.shape, lambda b: (0, 0)),
            pl.BlockSpec(wuv.shape, lambda b: (0, 0)),
        ],
        out_specs=pl.BlockSpec((sq, MLA_WIDTH), lambda b: (b, 0)),
        out_shape=jax.ShapeDtypeStruct((batch * sq, MLA_WIDTH), BF16),
        compiler_params=pltpu.CompilerParams(
            dimension_semantics=("parallel",), vmem_limit_bytes=V7X_VMEM_LIMIT_BYTES),
        name="attn_cache",
    )(q, cache_ckv, cache_kpe, new_ckv, new_kpe, wuk, wuv)


def _ffn_kernel(x_ref, gm_ref, mla_ref, cpast_ref, wo_ref, ln1g_ref, ln1b_ref, wup_ref, wconv_ref,
                bconv_ref, wdown_ref, ln2g_ref, ln2b_ref, y_ref, cstate_ref, carry_ref, act_ref,
                *, tm, seq_rows, tiles_per_seg, d_ff, cw, alpha):
    i = pl.program_id(0)
    n_seq = tm // seq_rows

    if n_seq == 1:
        @pl.when(i % tiles_per_seg == 0)
        def _():
            carry_ref[...] = cpast_ref[0]

    mix = _dot(gm_ref[...], wo_ref[0:GMLP_WIDTH, :]) + _dot(mla_ref[...], wo_ref[GMLP_WIDTH:, :])
    h = _layer_norm(alpha * x_ref[...] + mix, ln1g_ref[...], ln1b_ref[...])
    hb = h.astype(BF16)

    row = lax.broadcasted_iota(jnp.int32, (tm, cw), 0)

    def causal_conv(up, off):
        cols = slice(off, off + cw)
        s1, s2 = pltpu.roll(up, 1, 0), pltpu.roll(up, 2, 0)
        for k in range(n_seq):
            first = k * seq_rows
            history = carry_ref if n_seq == 1 else cpast_ref.at[k]
            prev2 = history[0:1, cols]
            prev1 = history[1:2, cols]
            s1 = jnp.where(row == first, prev1, s1)
            s2 = jnp.where(row == first, prev2, jnp.where(row == first + 1, prev1, s2))
            tail = up[first + seq_rows - (CONV_W - 1):first + seq_rows, :]
            if n_seq == 1:
                carry_ref[:, cols] = tail
            cstate_ref[k, :, cols] = tail
        conv = bconv_ref[:, cols] + wconv_ref[0:1, cols] * s2
        conv = conv + wconv_ref[1:2, cols] * s1
        return conv + wconv_ref[2:3, cols] * up

    for c in range(d_ff // cw):
        a = causal_conv(_dot(hb, wup_ref[:, c * cw:(c + 1) * cw]), c * cw)
        g = causal_conv(_dot(hb, wup_ref[:, d_ff + c * cw:d_ff + (c + 1) * cw]), d_ff + c * cw)
        act_ref[:, c * cw:(c + 1) * cw] = (a * (1.0 / (1.0 + jnp.exp(-a))) * g).astype(BF16)

    ff = _dot(act_ref[...], wdown_ref[...])
    y_ref[...] = _layer_norm(alpha * h + ff, ln2g_ref[...], ln2b_ref[...])


def _ffn_call(x, gm, mla, conv_past, wo, ln1g, ln1b, wup, wconv, bconv, wdown, ln2g, ln2b,
              *, tm, seq, alpha):
    n, d = x.shape
    d_ff = wdown.shape[0]
    batch = n // seq
    seq_rows = min(seq, tm)
    n_seq = tm // seq_rows
    tiles_per_seg = seq // seq_rows
    cw = FFN_COLUMN_CHUNK
    const = lambda *shape: pl.BlockSpec(shape, lambda i: (0,) * len(shape),
                                        pipeline_mode=pl.Buffered(1))
    return pl.pallas_call(
        functools.partial(_ffn_kernel, tm=tm, seq_rows=seq_rows, tiles_per_seg=tiles_per_seg, d_ff=d_ff,
                          cw=cw, alpha=alpha),
        grid=(n // tm,),
        in_specs=[
            pl.BlockSpec((tm, d), lambda i: (i, 0)),
            pl.BlockSpec((tm, GMLP_WIDTH), lambda i: (i, 0)),
            pl.BlockSpec((tm, MLA_WIDTH), lambda i: (i, 0)),
            pl.BlockSpec((n_seq, CONV_W - 1, 2 * d_ff), lambda i: (i // tiles_per_seg, 0, 0)),
            const(GMLP_WIDTH + MLA_WIDTH, d),
            const(1, d), const(1, d),
            const(d, 2 * d_ff),
            const(CONV_W, 2 * d_ff), const(1, 2 * d_ff),
            const(d_ff, d),
            const(1, d), const(1, d),
        ],
        out_specs=[
            pl.BlockSpec((tm, d), lambda i: (i, 0)),
            pl.BlockSpec((n_seq, CONV_W - 1, 2 * d_ff), lambda i: (i // tiles_per_seg, 0, 0)),
        ],
        out_shape=[
            jax.ShapeDtypeStruct((n, d), F32),
            jax.ShapeDtypeStruct((batch, CONV_W - 1, 2 * d_ff), F32),
        ],
        scratch_shapes=[
            pltpu.VMEM((CONV_W - 1, 2 * d_ff), F32),
            pltpu.VMEM((tm, d_ff), BF16),
        ],
        compiler_params=pltpu.CompilerParams(
            dimension_semantics=("arbitrary",), vmem_limit_bytes=V7X_VMEM_LIMIT_BYTES),
        name=f"ffn_tm{tm}",
    )(x, gm, mla, conv_past, wo, ln1g, ln1b, wup, wconv, bconv, wdown, ln2g, ln2b)


def _rope_tables(first_pos, n_pos, reps):
    inv_freq = jnp.power(ROPE_BASE, -jnp.arange(0, QK_ROPE_DIM, 2, dtype=F32) / QK_ROPE_DIM)
    ang = (first_pos + jnp.arange(n_pos)).astype(F32)[:, None] * inv_freq[None, :]
    cos, sin = jnp.cos(ang), jnp.sin(ang)
    return jnp.tile(jnp.concatenate([cos, cos, -sin, sin], axis=-1), (reps, 1))


def _swap_halves(w):
    half = w.shape[-1] // 2
    return jnp.concatenate([w[..., half:], w[..., :half]], axis=-1)


def _layer_weights(l, w_in, ln_v_g, ln_v_b, g_q, w_uq, g_kv, w_uk, w_uv, w_o, ln1_g, ln1_b, w_up,
                   w_conv, b_conv, w_down, ln2_g, ln2_b):
    row = lambda a: a[l].reshape(1, -1)
    o_kpe = 2 * GMLP_WIDTH + Q_LORA_RANK + KV_LORA_RANK
    w_in_ext = jnp.concatenate([w_in[l], _swap_halves(w_in[l][:, o_kpe:])], axis=1).astype(BF16)
    wq = w_uq[l]
    wuq_ext = jnp.concatenate([wq, _swap_halves(wq[..., QK_NOPE_DIM:])], axis=-1)
    wuq_ext = wuq_ext.reshape(Q_LORA_RANK, MLA_HEADS * Q_EXT_DIM).astype(BF16)
    return dict(
        w_in_ext=w_in_ext, lnv_g=row(ln_v_g), lnv_b=row(ln_v_b), g_q=row(g_q), wuq_ext=wuq_ext,
        g_kv=row(g_kv),
        wuk=w_uk[l].reshape(KV_LORA_RANK, MLA_HEADS * QK_NOPE_DIM).astype(BF16),
        wukt=w_uk[l].reshape(KV_LORA_RANK, MLA_HEADS * QK_NOPE_DIM).T.astype(BF16),
        wuv=w_uv[l].reshape(KV_LORA_RANK, MLA_WIDTH).astype(BF16),
        wo=w_o[l].astype(BF16), ln1g=row(ln1_g), ln1b=row(ln1_b), wup=w_up[l].astype(BF16),
        wconv=w_conv[l], bconv=row(b_conv), wdown=w_down[l].astype(BF16), ln2g=row(ln2_g),
        ln2b=row(ln2_b))


def _mix_operands(w_s, b_s, seg):
    reps = GMLP_CHUNK // seg
    wmix = jnp.tile(w_s[:, :seg, :seg], (1, reps, reps))
    bcol = jnp.tile(b_s[:, :seg].T, (reps, 1))
    return wmix, jnp.repeat(bcol, GMLP_HEAD_DIM, axis=1)


def _tiles(batch, seq):
    if seq >= PROMPT_TILE:
        return dict(proj=PROMPT_TILE, attn=ATTN_QUERY_TILE, ffn=PROMPT_TILE)
    assert (batch * seq) % GMLP_CHUNK == 0 and GMLP_CHUNK % seq == 0
    return dict(proj=batch * seq, attn=None, ffn=batch * seq)


def _trunk(x, w, w_s, b_s, *, batch, seq, past, conv_past, alpha, cache=None):
    tiles = _tiles(batch, seq)
    seg = min(seq, GMLP_CHUNK)
    wmix, bmix = _mix_operands(w_s, b_s, seg)
    rope_t = _rope_tables(past, seq, max(1, tiles["proj"] // seq))
    gm, q, kt, v, ckv, kpe, *vln = _proj_call(
        x, w["w_in_ext"], w["lnv_g"], w["lnv_b"], wmix, bmix, w["g_q"], w["wuq_ext"], w["g_kv"],
        w["wukt"], w["wuv"], rope_t, tm=tiles["proj"], seg=seg, emit_vln=cache is not None)
    if cache is None:
        mla = _attn_call(q, kt, v, batch=batch, seq=seq, tq=tiles["attn"], tk=tiles["proj"],
                         heads=ATTN_HEADS_PER_STEP)
    else:
        mla = _attn_cache_call(q, cache[0], cache[1], ckv, kpe, w["wuk"], w["wuv"], batch=batch, sq=seq)
    y, conv_state = _ffn_call(x, gm, mla, conv_past, w["wo"], w["ln1g"], w["ln1b"], w["wup"],
                              w["wconv"], w["bconv"], w["wdown"], w["ln2g"], w["ln2b"],
                              tm=tiles["ffn"], seq=seq, alpha=alpha)
    return y, ckv, kpe, (vln[0] if vln else None), conv_state


def kernel(x_prompt, x_sample, cache_ckv, cache_kpe, state_ffn_conv, w_in, ln_v_g, ln_v_b, w_s, b_s, g_q, w_uq, g_kv, w_uk, w_uv, w_o, ln1_g, ln1_b, w_up, w_conv, b_conv, w_down, ln2_g, ln2_b):
    depth = w_in.shape[0]
    alpha = (2.0 * depth) ** 0.25
    bp, sp, d = x_prompt.shape
    bs, ss, _ = x_sample.shape
    d_ff = w_down.shape[1]
    hp = x_prompt.reshape(bp * sp, d)
    hs = x_sample.reshape(bs * ss, d)
    outs = [[] for _ in range(7)]
    for l in range(depth):
        w = _layer_weights(l, w_in, ln_v_g, ln_v_b, g_q, w_uq, g_kv, w_uk, w_uv, w_o, ln1_g, ln1_b,
                           w_up, w_conv, b_conv, w_down, ln2_g, ln2_b)
        hp, ckv_p, kpe_p, _, conv_p = _trunk(
            hp, w, w_s[l], b_s[l], batch=bp, seq=sp, past=0,
            conv_past=jnp.zeros((bp, CONV_W - 1, 2 * d_ff), F32), alpha=alpha)
        hs, ckv_s, kpe_s, v_s, conv_s = _trunk(
            hs, w, w_s[l], b_s[l], batch=bs, seq=ss, past=cache_ckv.shape[2],
            conv_past=state_ffn_conv[l], alpha=alpha, cache=(cache_ckv[l], cache_kpe[l]))
        for acc, val in zip(outs, (
                ckv_p.reshape(bp, sp, -1), kpe_p.reshape(bp, sp, -1), conv_p,
                ckv_s.reshape(bs, ss, -1), kpe_s.reshape(bs, ss, -1), v_s.reshape(bs, ss, -1), conv_s)):
            acc.append(val)
    return (hp.reshape(bp, sp, d), hs.reshape(bs, ss, d), *(jnp.stack(o) for o in outs))
```
